```python
import math
import jax, jax.numpy as jnp
from jax import lax
import numpy as np

D_MODEL = 1024
BATCH = 8
SEQ = 8192
DEPTH = 1

N_META = 16
BLOCK = 128
META_PAD = (-N_META) % BLOCK
MLA_HEADS = 4
QK_NOPE = 128
QK_ROPE = 64
V_DIM = 128
Q_LORA = 256
KV_LORA = 256
ROPE_THETA = 10000.0
SB_HEADS = 8
SB_DIM = 64
MLA_W = MLA_HEADS * V_DIM
SB_W = SB_HEADS * SB_DIM
MIX_W = MLA_W + SB_W
IN_SPLITS = (Q_LORA, KV_LORA, QK_ROPE, SB_W, SB_W, SB_W)
IN_W = Q_LORA + KV_LORA + QK_ROPE + 3 * SB_W
N_EXPERTS = 32
TOP_K = 4
D_FF = 1024
SWIGLU_LIMIT = 7.0
SWIGLU_ALPHA = 1.702
EXPERT_BLOCK = 256
DN_ALPHA = (2 * DEPTH) ** 0.25
DN_BETA = (8 * DEPTH) ** -0.25
LN_EPS = 1e-5
RMS_EPS = 1e-6
NEG_INF = -1e30

kernel_name = 'hybrid_mla_stickbreak_moe_deepnorm'


def rms_norm(x, g):
    xf = x.astype(jnp.float32)
    y = xf * lax.rsqrt(jnp.mean(xf * xf, axis=-1, keepdims=True) + RMS_EPS) * g.astype(jnp.float32)
    return y.astype(x.dtype)


def layer_norm(x, g, b):
    xf = x.astype(jnp.float32)
    mu = jnp.mean(xf, axis=-1, keepdims=True)
    xc = xf - mu
    var = jnp.mean(xc * xc, axis=-1, keepdims=True)
    y = xc * lax.rsqrt(var + LN_EPS) * g.astype(jnp.float32) + b.astype(jnp.float32)
    return y.astype(x.dtype)


def rope_tables(length):
    half = QK_ROPE // 2
    freqs = ROPE_THETA ** (-jnp.arange(half, dtype=jnp.float32) * 2.0 / QK_ROPE)
    ang = jnp.arange(length, dtype=jnp.float32)[:, None] * freqs[None, :]
    return jnp.cos(ang)[:, None, :], jnp.sin(ang)[:, None, :]


def rope(x, cos, sin):
    half = x.shape[-1] // 2
    x1 = x[..., :half].astype(jnp.float32)
    x2 = x[..., half:].astype(jnp.float32)
    return jnp.concatenate([x1 * cos - x2 * sin, x2 * cos + x1 * sin], axis=-1).astype(x.dtype)


def to_blocks(t):
    b, lp, h, d = t.shape
    return t.reshape(b, lp // BLOCK, BLOCK, h, d).transpose(1, 0, 2, 3, 4)


def from_blocks(t):
    nb, b, q, h, d = t.shape
    return t.transpose(1, 0, 2, 3, 4).reshape(b, nb * q, h, d)


def mla_attention(q, k, v):
    lp = q.shape[1]
    k_pos = jnp.arange(lp)
    scale = (QK_NOPE + QK_ROPE) ** -0.5

    def one_block(args):
        qb, bi = args
        q_pos = bi * BLOCK + jnp.arange(BLOCK)
        valid = (k_pos[None, :] <= q_pos[:, None]) & (k_pos[None, :] >= META_PAD)
        s = jnp.einsum('bqhd,bkhd->bhqk', qb, k).astype(jnp.float32) * scale
        p = jax.nn.softmax(jnp.where(valid, s, NEG_INF), axis=-1)
        return jnp.einsum('bhqk,bkhd->bqhd', p.astype(v.dtype), v)

    out = lax.map(one_block, (to_blocks(q), jnp.arange(lp // BLOCK)))
    return from_blocks(out)


def stick_breaking_attention(q, k, v):
    lp = q.shape[1]
    k_pos = jnp.arange(lp)
    scale = SB_DIM ** -0.5

    def one_block(args):
        qb, bi = args
        q_pos = bi * BLOCK + jnp.arange(BLOCK)
        valid = (k_pos[None, :] < q_pos[:, None]) & (k_pos[None, :] >= META_PAD)
        z = jnp.einsum('bqhd,bkhd->bhqk', qb, k).astype(jnp.float32) * scale
        log_keep = jnp.where(valid, jax.nn.log_sigmoid(-z), 0.0)
        later = lax.cumsum(log_keep, axis=3, reverse=True) - log_keep
        w = jnp.where(valid, jnp.exp(jax.nn.log_sigmoid(z) + later), 0.0)
        return jnp.einsum('bhqk,bkhd->bqhd', w.astype(v.dtype), v)

    out = lax.map(one_block, (to_blocks(q), jnp.arange(lp // BLOCK)))
    return from_blocks(out)


def token_mixing(h, cos, sin, w_in, q_norm_g, w_uq, kv_norm_g, w_ukv, mla_out_g, sb_out_g, w_o):
    b, l, _ = h.shape
    split_at = np.cumsum(IN_SPLITS)[:-1].tolist()
    c_q, c_kv, k_r, sb_q, sb_k, sb_v = jnp.split(h @ w_in, split_at, axis=-1)
    q = (rms_norm(c_q, q_norm_g) @ w_uq).reshape(b, l, MLA_HEADS, QK_NOPE + QK_ROPE)
    q = jnp.concatenate([q[..., :QK_NOPE], rope(q[..., QK_NOPE:], cos, sin)], axis=-1)
    kv = (rms_norm(c_kv, kv_norm_g) @ w_ukv).reshape(b, l, MLA_HEADS, QK_NOPE + V_DIM)
    k_rope = rope(k_r.reshape(b, l, 1, QK_ROPE), cos, sin)
    k = jnp.concatenate([kv[..., :QK_NOPE], jnp.broadcast_to(k_rope, (b, l, MLA_HEADS, QK_ROPE))], axis=-1)
    v = kv[..., QK_NOPE:]

    def pad(t):
        return jnp.pad(t, ((0, 0), (META_PAD, 0), (0, 0), (0, 0)))

    o_mla = mla_attention(pad(q), pad(k), pad(v))[:, META_PAD:].reshape(b, l, MLA_W)
    sq = sb_q.reshape(b, l, SB_HEADS, SB_DIM)
    sk = sb_k.reshape(b, l, SB_HEADS, SB_DIM)
    sv = sb_v.reshape(b, l, SB_HEADS, SB_DIM)
    o_sb = stick_breaking_attention(pad(sq), pad(sk), pad(sv))[:, META_PAD:].reshape(b, l, SB_W)
    o = jnp.concatenate([rms_norm(o_mla, mla_out_g), rms_norm(o_sb, sb_out_g)], axis=-1)
    return o @ w_o


def moe_ffn(h, w_router, b_router, w_gate, b_gate, w_up, b_up, w_down, b_down):
    t, d = h.shape
    tk = t * TOP_K
    logits = (h @ w_router + b_router).astype(jnp.float32)
    top_v, top_i = lax.top_k(logits, TOP_K)
    gates = jax.nn.softmax(top_v, axis=-1).astype(h.dtype)
    flat_e = top_i.reshape(tk)
    order = jnp.argsort(flat_e)
    s_e = flat_e[order]
    s_tok = order // TOP_K
    s_gate = gates.reshape(tk)[order]
    counts = jnp.bincount(flat_e, length=N_EXPERTS)
    padded = (counts + EXPERT_BLOCK - 1) // EXPERT_BLOCK * EXPERT_BLOCK
    start = jnp.cumsum(counts) - counts
    p_end = jnp.cumsum(padded)
    p_start = p_end - padded
    dest = p_start[s_e] + jnp.arange(tk) - start[s_e]
    n_blk = (tk + N_EXPERTS * (EXPERT_BLOCK - 1) + EXPERT_BLOCK - 1) // EXPERT_BLOCK
    buf = jnp.zeros((n_blk * EXPERT_BLOCK, d), h.dtype).at[dest].set(h[s_tok])
    blk_e = jnp.minimum(jnp.sum(jnp.arange(n_blk)[:, None] * EXPERT_BLOCK >= p_end[None, :], axis=1), N_EXPERTS - 1)

    def expert_block(args):
        xb, e = args
        g = jnp.minimum(xb @ w_gate[e] + b_gate[e], SWIGLU_LIMIT)
        u = jnp.clip(xb @ w_up[e] + b_up[e], -SWIGLU_LIMIT, SWIGLU_LIMIT)
        a = (u + 1.0) * (g * jax.nn.sigmoid(SWIGLU_ALPHA * g))
        return a @ w_down[e] + b_down[e]

    y_buf = lax.map(expert_block, (buf.reshape(n_blk, EXPERT_BLOCK, d), blk_e)).reshape(n_blk * EXPERT_BLOCK, d)
    y = y_buf[dest] * s_gate[:, None]
    return jax.ops.segment_sum(y, s_tok, num_segments=t)


def setup_inputs(seed: int = 0) -> dict:
    key = jax.random.key(seed)
    ks = jax.random.split(key, 24)
    f32 = jnp.float32

    def nrm(k, shape, scale):
        return jax.random.normal(k, shape, f32) * scale

    return {
        'x': nrm(ks[0], (BATCH, SEQ, D_MODEL), 1.0),
        'meta_tokens': nrm(ks[1], (N_META, D_MODEL), 1.0),
        'w_in': nrm(ks[2], (DEPTH, D_MODEL, IN_W), D_MODEL ** -0.5),
        'q_norm_g': 1.0 + nrm(ks[3], (DEPTH, Q_LORA), 0.02),
        'w_uq': nrm(ks[4], (DEPTH, Q_LORA, MLA_HEADS * (QK_NOPE + QK_ROPE)), Q_LORA ** -0.5),
        'kv_norm_g': 1.0 + nrm(ks[5], (DEPTH, KV_LORA), 0.02),
        'w_ukv': nrm(ks[6], (DEPTH, KV_LORA, MLA_HEADS * (QK_NOPE + V_DIM)), KV_LORA ** -0.5),
        'mla_out_g': 1.0 + nrm(ks[7], (DEPTH, MLA_W), 0.02),
        'sb_out_g': 1.0 + nrm(ks[8], (DEPTH, SB_W), 0.02),
        'w_o': nrm(ks[9], (DEPTH, MIX_W, D_MODEL), DN_BETA * MIX_W ** -0.5),
        'ln1_g': 1.0 + nrm(ks[10], (DEPTH, D_MODEL), 0.02),
        'ln1_b': nrm(ks[11], (DEPTH, D_MODEL), 0.01),
        'w_router': nrm(ks[12], (DEPTH, D_MODEL, N_EXPERTS), D_MODEL ** -0.5),
        'b_router': nrm(ks[13], (DEPTH, N_EXPERTS), 0.01),
        'w_gate': nrm(ks[14], (DEPTH, N_EXPERTS, D_MODEL, D_FF), D_MODEL ** -0.5),
        'b_gate': nrm(ks[15], (DEPTH, N_EXPERTS, D_FF), 0.01),
        'w_up': nrm(ks[16], (DEPTH, N_EXPERTS, D_MODEL, D_FF), D_MODEL ** -0.5),
        'b_up': nrm(ks[17], (DEPTH, N_EXPERTS, D_FF), 0.01),
        'w_down': nrm(ks[18], (DEPTH, N_EXPERTS, D_FF, D_MODEL), DN_BETA * D_FF ** -0.5),
        'b_down': nrm(ks[19], (DEPTH, N_EXPERTS, D_MODEL), 0.01),
        'ln2_g': 1.0 + nrm(ks[20], (DEPTH, D_MODEL), 0.02),
        'ln2_b': nrm(ks[21], (DEPTH, D_MODEL), 0.01),
    }


def reference(x, meta_tokens, w_in, q_norm_g, w_uq, kv_norm_g, w_ukv, mla_out_g, sb_out_g, w_o,
              ln1_g, ln1_b, w_router, b_router, w_gate, b_gate, w_up, b_up, w_down, b_down,
              ln2_g, ln2_b):
    b = x.shape[0]
    meta = jnp.broadcast_to(meta_tokens[None].astype(x.dtype), (b, N_META, D_MODEL))
    h = jnp.concatenate([meta, x], axis=1)
    l = h.shape[1]
    cos, sin = rope_tables(l)
    for i in range(DEPTH):
        a = token_mixing(h, cos, sin, w_in[i], q_norm_g[i], w_uq[i], kv_norm_g[i], w_ukv[i],
                         mla_out_g[i], sb_out_g[i], w_o[i])
        h = layer_norm(DN_ALPHA * h + a, ln1_g[i], ln1_b[i])
        f = moe_ffn(h.reshape(b * l, D_MODEL), w_router[i], b_router[i], w_gate[i], b_gate[i],
                    w_up[i], b_up[i], w_down[i], b_down[i]).reshape(b, l, D_MODEL)
        h = layer_norm(DN_ALPHA * h + f, ln2_g[i], ln2_b[i])
    return h[:, N_META:]
```

```python
import functools

import jax
import jax.numpy as jnp
from jax import lax
from jax.experimental import pallas as pl
from jax.experimental.pallas import tpu as pltpu

F32 = jnp.float32
BF16 = jnp.bfloat16
I32 = jnp.int32

D_MODEL = 1024
N_META = 16
MLA_HEADS = 4
QK_NOPE = 128
QK_ROPE = 64
V_DIM = 128
Q_LORA = 256
KV_LORA = 256
ROPE_THETA = 10000.0
SB_HEADS = 8
SB_DIM = 64
MLA_W = MLA_HEADS * V_DIM
SB_W = SB_HEADS * SB_DIM
N_EXPERTS = 32
TOP_K = 4
D_FF = 1024
SWIGLU_LIMIT = 7.0
SWIGLU_ALPHA = 1.702
DEPTH = 1
DN_ALPHA = (2 * DEPTH) ** 0.25
LN_EPS = 1e-5
RMS_EPS = 1e-6
NEG_INF = -1e30

LANES = 128
META_ROWS = 128
QK_PAD = 2 * LANES
ROW_BLOCK = 256
VMEM_LIMIT = 56 * 1024 * 1024


def _cparams(sem):
    return pltpu.CompilerParams(dimension_semantics=sem, vmem_limit_bytes=VMEM_LIMIT)


def _rms(x, g):
    return x * lax.rsqrt(jnp.mean(x * x, axis=-1, keepdims=True) + RMS_EPS) * g


def _layer_norm(x, g, b):
    mu = jnp.mean(x, axis=-1, keepdims=True)
    xc = x - mu
    var = jnp.mean(xc * xc, axis=-1, keepdims=True)
    return xc * lax.rsqrt(var + LN_EPS) * g + b


def _dot(a, b):
    return jnp.dot(a, b, preferred_element_type=F32)


def _dot_nt(a, b):
    return lax.dot_general(a, b, (((1,), (1,)), ((), ())), preferred_element_type=F32)


def _proj_kernel(x_ref, w1_ref, qg_ref, kvg_ref, wq_ref, wkv_ref, cos_ref, sin_ref,
                 qm_ref, km_ref, vm_ref, sq_ref, sk_ref, sv_ref):
    xb = x_ref[0].astype(BF16)
    p = _dot(xb, w1_ref[...])
    sq_ref[0] = p[:, 512:1024].astype(BF16)
    sk_ref[0] = p[:, 1024:1536].astype(BF16)
    sv_ref[0] = p[:, 1536:2048].astype(BF16)
    cos = cos_ref[...]
    sin = sin_ref[...]
    k_rope = (p[:, 2048:2176] * cos + p[:, 2176:2304] * sin).astype(BF16)
    nq = _rms(p[:, 0:256], qg_ref[...]).astype(BF16)
    nkv = _rms(p[:, 256:512], kvg_ref[...]).astype(BF16)
    qq = _dot(nq, wq_ref[...])
    kv = _dot(nkv, wkv_ref[...])
    scale = (QK_NOPE + QK_ROPE) ** -0.5
    for h in range(MLA_HEADS):
        lo = h * QK_PAD
        qm_ref[0, :, lo:lo + LANES] = (qq[:, h * LANES:(h + 1) * LANES] * scale).astype(BF16)
        q_rope = qq[:, 512 + h * LANES:512 + (h + 1) * LANES] * cos + qq[:, 1024 + h * LANES:1024 + (h + 1) * LANES] * sin
        qm_ref[0, :, lo + LANES:lo + QK_PAD] = (q_rope * scale).astype(BF16)
        km_ref[0, :, lo:lo + LANES] = kv[:, h * LANES:(h + 1) * LANES].astype(BF16)
        km_ref[0, :, lo + LANES:lo + QK_PAD] = k_rope
    vm_ref[0] = kv[:, 512:1024].astype(BF16)


def _project(x, w1, qg, kvg, wq, wkv, cos, sin, tm):
    b, l, d = x.shape
    full = lambda a: pl.BlockSpec(a.shape, lambda bi, i: (0,) * a.ndim)
    tok = lambda w: pl.BlockSpec((1, tm, w), lambda bi, i: (bi, i, 0))
    tab = pl.BlockSpec((tm, LANES), lambda bi, i: (i, 0))
    widths = (MLA_HEADS * QK_PAD, MLA_HEADS * QK_PAD, MLA_W, SB_W, SB_W, SB_W)
    return pl.pallas_call(
        _proj_kernel,
        grid=(b, l // tm),
        in_specs=[tok(d), full(w1), full(qg), full(kvg), full(wq), full(wkv), tab, tab],
        out_specs=[tok(w) for w in widths],
        out_shape=[jax.ShapeDtypeStruct((b, l, w), BF16) for w in widths],
        compiler_params=_cparams(("parallel", "parallel")),
        name="proj",
    )(x, w1, qg, kvg, wq, wkv, cos, sin)


def _mla_kernel(q_ref, k_ref, v_ref, kmeta_ref, vmeta_ref, o_ref, m_sc, l_sc, acc_sc, *, tq):
    qi = pl.program_id(2)
    q = q_ref[0]

    s = _dot_nt(q, kmeta_ref[0])
    col = lax.broadcasted_iota(I32, s.shape, 1)
    s = jnp.where(col < N_META, s, NEG_INF)
    m0 = jnp.max(s, axis=1, keepdims=True)
    p0 = jnp.exp(s - m0)
    m_sc[...] = m0
    l_sc[...] = jnp.sum(p0, axis=1, keepdims=True)
    acc_sc[...] = _dot(p0.astype(BF16), vmeta_ref[0])

    def update(s, vblk):
        m_prev = m_sc[...]
        m_new = jnp.maximum(m_prev, jnp.max(s, axis=1, keepdims=True))
        alpha = jnp.exp(m_prev - m_new)
        p = jnp.exp(s - m_new)
        l_sc[...] = alpha * l_sc[...] + jnp.sum(p, axis=1, keepdims=True)
        acc_sc[...] = alpha * acc_sc[...] + _dot(p.astype(BF16), vblk)
        m_sc[...] = m_new

    def body(ki, carry):
        start = pl.multiple_of(ki * tq, tq)
        update(_dot_nt(q, k_ref[0, pl.ds(start, tq), :]), v_ref[0, pl.ds(start, tq), :])
        return carry

    lax.fori_loop(0, qi, body, 0)

    start = pl.multiple_of(qi * tq, tq)
    s = _dot_nt(q, k_ref[0, pl.ds(start, tq), :])
    row = lax.broadcasted_iota(I32, s.shape, 0)
    col = lax.broadcasted_iota(I32, s.shape, 1)
    update(jnp.where(col <= row, s, NEG_INF), v_ref[0, pl.ds(start, tq), :])
    o_ref[0] = acc_sc[...] / l_sc[...]


def _mla_attention(q, k, v, kmeta, vmeta, tq):
    b, l, _ = q.shape
    return pl.pallas_call(
        functools.partial(_mla_kernel, tq=tq),
        grid=(b, MLA_HEADS, l // tq),
        in_specs=[
            pl.BlockSpec((1, tq, QK_PAD), lambda bi, h, i: (bi, i, h)),
            pl.BlockSpec((1, l, QK_PAD), lambda bi, h, i: (bi, 0, h)),
            pl.BlockSpec((1, l, V_DIM), lambda bi, h, i: (bi, 0, h)),
            pl.BlockSpec((1, META_ROWS, QK_PAD), lambda bi, h, i: (0, 0, h)),
            pl.BlockSpec((1, META_ROWS, V_DIM), lambda bi, h, i: (0, 0, h)),
        ],
        out_specs=pl.BlockSpec((1, tq, V_DIM), lambda bi, h, i: (bi, i, h)),
        out_shape=jax.ShapeDtypeStruct((b, l, MLA_W), F32),
        scratch_shapes=[pltpu.VMEM((tq, 1), F32), pltpu.VMEM((tq, 1), F32), pltpu.VMEM((tq, V_DIM), F32)],
        compiler_params=_cparams(("parallel", "parallel", "arbitrary")),
        name="mla",
    )(q, k, v, kmeta, vmeta)


def _sb_kernel(q_ref, k_ref, v_ref, kmeta_ref, vmeta_ref, o_ref, c_sc, acc_sc, *, tq):
    qi = pl.program_id(2)
    q = q_ref[0]
    lane = lax.broadcasted_iota(I32, q.shape, 1)
    zero = jnp.zeros_like(q)
    q_heads = (jnp.where(lane < SB_DIM, q, zero), jnp.where(lane >= SB_DIM, q, zero))

    def later_matrix(n):
        r = lax.broadcasted_iota(I32, (n, n), 0)
        c = lax.broadcasted_iota(I32, (n, n), 1)
        return jnp.where(r > c, 1.0, 0.0).astype(BF16)

    c_sc[...] = jnp.zeros_like(c_sc)
    acc_sc[...] = jnp.zeros_like(acc_sc)

    def block(kblk, vblk, valid, u):
        for h in range(2):
            z = _dot_nt(q_heads[h], kblk)
            softplus = jnp.maximum(z, 0.0) + jnp.log(1.0 + jnp.exp(-jnp.abs(z)))
            log_keep = -softplus
            if valid is not None:
                log_keep = jnp.where(valid, log_keep, 0.0)
            hi = log_keep.astype(BF16)
            lo = (log_keep - hi.astype(F32)).astype(BF16)
            later = _dot(hi, u) + _dot(lo, u) + c_sc[h]
            w = jnp.exp(z - softplus + later)
            if valid is not None:
                w = jnp.where(valid, w, 0.0)
            acc_sc[h] += _dot(w.astype(BF16), vblk)
            c_sc[h] = later[:, 0:1] + log_keep[:, 0:1]

    u_blk = later_matrix(tq)
    start = pl.multiple_of(qi * tq, tq)
    row = lax.broadcasted_iota(I32, (tq, tq), 0)
    col = lax.broadcasted_iota(I32, (tq, tq), 1)
    block(k_ref[0, pl.ds(start, tq), :], v_ref[0, pl.ds(start, tq), :], col < row, u_blk)

    def body(i, carry):
        st = pl.multiple_of((qi - 1 - i) * tq, tq)
        block(k_ref[0, pl.ds(st, tq), :], v_ref[0, pl.ds(st, tq), :], None, u_blk)
        return carry

    lax.fori_loop(0, qi, body, 0)

    mcol = lax.broadcasted_iota(I32, (tq, META_ROWS), 1)
    block(kmeta_ref[0], vmeta_ref[0], mcol < N_META, later_matrix(META_ROWS))
    o_ref[0] = jnp.where(lane < SB_DIM, acc_sc[0], acc_sc[1])


def _sb_attention(q, k, v, kmeta, vmeta, tq):
    b, l, _ = q.shape
    pairs = SB_W // LANES
    return pl.pallas_call(
        functools.partial(_sb_kernel, tq=tq),
        grid=(b, pairs, l // tq),
        in_specs=[
            pl.BlockSpec((1, tq, LANES), lambda bi, h, i: (bi, i, h)),
            pl.BlockSpec((1, l, LANES), lambda bi, h, i: (bi, 0, h)),
            pl.BlockSpec((1, l, LANES), lambda bi, h, i: (bi, 0, h)),
            pl.BlockSpec((1, META_ROWS, LANES), lambda bi, h, i: (0, 0, h)),
            pl.BlockSpec((1, META_ROWS, LANES), lambda bi, h, i: (0, 0, h)),
        ],
        out_specs=pl.BlockSpec((1, tq, LANES), lambda bi, h, i: (bi, i, h)),
        out_shape=jax.ShapeDtypeStruct((b, l, SB_W), F32),
        scratch_shapes=[pltpu.VMEM((2, tq, 1), F32), pltpu.VMEM((2, tq, LANES), F32)],
        compiler_params=_cparams(("parallel", "parallel", "arbitrary")),
        name="sb",
    )(q, k, v, kmeta, vmeta)


def _post_kernel(x_ref, om_ref, os_ref, g1_ref, g2_ref, wo_ref, lg_ref, lb_ref, wr_ref, br_ref,
                 h1_ref, h1b_ref, lt_ref):
    n1 = _rms(om_ref[...], g1_ref[...]).astype(BF16)
    n2 = _rms(os_ref[...], g2_ref[...]).astype(BF16)
    a = _dot(n1, wo_ref[0:MLA_W, :]) + _dot(n2, wo_ref[MLA_W:MLA_W + SB_W, :])
    h1 = _layer_norm(DN_ALPHA * x_ref[...] + a, lg_ref[...], lb_ref[...])
    h1_ref[...] = h1
    h1b_ref[...] = h1.astype(BF16)
    lt_ref[...] = lax.dot_general(wr_ref[...], h1, (((1,), (1,)), ((), ())),
                                  precision=lax.Precision.HIGHEST, preferred_element_type=F32) + br_ref[...]


def _post(x2, om, osb, g1, g2, wo, lg, lb, wr_t, br, tm):
    t, d = x2.shape
    full = lambda a: pl.BlockSpec(a.shape, lambda i: (0,) * a.ndim)
    tok = lambda w: pl.BlockSpec((tm, w), lambda i: (i, 0))
    return pl.pallas_call(
        _post_kernel,
        grid=(t // tm,),
        in_specs=[tok(d), tok(MLA_W), tok(SB_W), full(g1), full(g2), full(wo), full(lg), full(lb), full(wr_t), full(br)],
        out_specs=[tok(d), tok(d), pl.BlockSpec((N_EXPERTS, tm), lambda i: (0, i))],
        out_shape=[jax.ShapeDtypeStruct((t, d), F32), jax.ShapeDtypeStruct((t, d), BF16),
                   jax.ShapeDtypeStruct((N_EXPERTS, t), F32)],
        compiler_params=_cparams(("parallel",)),
        name="post",
    )(x2, om, osb, g1, g2, wo, lg, lb, wr_t, br)


def _route_kernel(lt_ref, ti_ref, gate_ref, rank_ref, cnt_ref, carry_sc, *, tt):
    @pl.when(pl.program_id(0) == 0)
    def _():
        carry_sc[...] = jnp.zeros_like(carry_sc)

    logits = lt_ref[...]
    eidx = lax.broadcasted_iota(I32, logits.shape, 0)
    sels, vals = [], []
    for k in range(TOP_K):
        mx = jnp.max(logits, axis=0, keepdims=True)
        idx = jnp.min(jnp.where(logits == mx, eidx, N_EXPERTS), axis=0, keepdims=True)
        sel = eidx == idx
        logits = jnp.where(sel, -jnp.inf, logits)
        ti_ref[k:k + 1, :] = idx
        sels.append(sel)
        vals.append(mx)
    exps = [jnp.exp(v - vals[0]) for v in vals]
    denom = exps[0] + exps[1] + exps[2] + exps[3]
    for k in range(TOP_K):
        gate_ref[k:k + 1, :] = exps[k] / denom

    chosen = sum(jnp.where(s, 1.0, 0.0) for s in sels)
    r = lax.broadcasted_iota(I32, (tt, tt), 0)
    c = lax.broadcasted_iota(I32, (tt, tt), 1)
    before = jnp.where(r < c, 1.0, 0.0).astype(BF16)
    rank = _dot(chosen.astype(BF16), before) + carry_sc[...]
    for k in range(TOP_K):
        rank_ref[k:k + 1, :] = jnp.sum(jnp.where(sels[k], rank, 0.0), axis=0, keepdims=True).astype(I32)
    carry_sc[...] += jnp.sum(chosen, axis=1, keepdims=True)
    cnt_ref[...] = jnp.broadcast_to(carry_sc[...], cnt_ref.shape)


def _route(lt, tt):
    e, t = lt.shape
    tok = pl.BlockSpec((TOP_K, tt), lambda i: (0, i))
    return pl.pallas_call(
        functools.partial(_route_kernel, tt=tt),
        grid=(t // tt,),
        in_specs=[pl.BlockSpec((e, tt), lambda i: (0, i))],
        out_specs=[tok, tok, tok, pl.BlockSpec((e, LANES), lambda i: (0, 0))],
        out_shape=[jax.ShapeDtypeStruct((TOP_K, t), I32), jax.ShapeDtypeStruct((TOP_K, t), F32),
                   jax.ShapeDtypeStruct((TOP_K, t), I32), jax.ShapeDtypeStruct((e, LANES), F32)],
        scratch_shapes=[pltpu.VMEM((e, 1), F32)],
        compiler_params=_cparams(("arbitrary",)),
        name="route",
    )(lt)


def _expert_kernel(be_ref, nu_ref, x_ref, wg_ref, bg_ref, wu_ref, bu_ref, wd_ref, bd_ref, y_ref):
    i = pl.program_id(0)

    @pl.when(i < nu_ref[0])
    def _():
        x = x_ref[...]
        g = jnp.minimum(_dot(x, wg_ref[0]) + bg_ref[0], SWIGLU_LIMIT)
        u = jnp.clip(_dot(x, wu_ref[0]) + bu_ref[0], -SWIGLU_LIMIT, SWIGLU_LIMIT)
        a = (u + 1.0) * (g * (1.0 / (1.0 + jnp.exp(-SWIGLU_ALPHA * g))))
        y_ref[...] = (_dot(a.astype(BF16), wd_ref[0]) + bd_ref[0]).astype(y_ref.dtype)

    @pl.when(i >= nu_ref[0])
    def _():
        y_ref[...] = jnp.zeros_like(y_ref)


def _experts(blk_e, n_used, buf, wg, bg, wu, bu, wd, bd):
    n_rows, d = buf.shape
    n_blk = n_rows // ROW_BLOCK
    wspec = lambda a: pl.BlockSpec((1,) + a.shape[1:], lambda i, be, nu: (be[i], 0, 0))
    rows = pl.BlockSpec((ROW_BLOCK, d), lambda i, be, nu: (i, 0))
    return pl.pallas_call(
        _expert_kernel,
        grid_spec=pltpu.PrefetchScalarGridSpec(
            num_scalar_prefetch=2,
            grid=(n_blk,),
            in_specs=[rows, wspec(wg), wspec(bg), wspec(wu), wspec(bu), wspec(wd), wspec(bd)],
            out_specs=rows,
        ),
        out_shape=jax.ShapeDtypeStruct((n_rows, d), BF16),
        compiler_params=_cparams(("arbitrary",)),
        name="experts",
    )(blk_e, n_used, buf, wg, bg, wu, bu, wd, bd)


def _combine_kernel(h1_ref, y_ref, gate_ref, lg_ref, lb_ref, o_ref):
    gates = gate_ref[...]
    f = gates[:, 0:1] * y_ref[0].astype(F32)
    for k in range(1, TOP_K):
        f = f + gates[:, k:k + 1] * y_ref[k].astype(F32)
    o_ref[...] = _layer_norm(DN_ALPHA * h1_ref[...] + f, lg_ref[...], lb_ref[...])


def _combine(h1, yg, gates, lg, lb, tm):
    t, d = h1.shape
    full = lambda a: pl.BlockSpec(a.shape, lambda i: (0,) * a.ndim)
    return pl.pallas_call(
        _combine_kernel,
        grid=(t // tm,),
        in_specs=[pl.BlockSpec((tm, d), lambda i: (i, 0)), pl.BlockSpec((TOP_K, tm, d), lambda i: (0, i, 0)),
                  pl.BlockSpec((tm, TOP_K), lambda i: (i, 0)), full(lg), full(lb)],
        out_specs=pl.BlockSpec((tm, d), lambda i: (i, 0)),
        out_shape=jax.ShapeDtypeStruct((t, d), F32),
        compiler_params=_cparams(("parallel",)),
        name="combine",
    )(h1, yg, gates, lg, lb)


def _rope_tables(positions):
    half = QK_ROPE // 2
    freqs = ROPE_THETA ** (-jnp.arange(half, dtype=F32) * 2.0 / QK_ROPE)
    ang = positions.astype(F32)[:, None] * freqs[None, :]
    pad = jnp.zeros((positions.shape[0], LANES - QK_ROPE), F32)
    cos, sin = jnp.cos(ang), jnp.sin(ang)
    return jnp.concatenate([cos, cos, pad], axis=1), jnp.concatenate([sin, sin, pad], axis=1)


def _rope_cols(w):
    half = QK_ROPE // 2
    x1, x2 = w[:, :half], w[:, half:]
    pad = jnp.zeros((w.shape[0], LANES - QK_ROPE), w.dtype)
    return jnp.concatenate([x1, x2, pad], axis=1), jnp.concatenate([-x2, x1, pad], axis=1)


def _layout_weights(w_in, w_uq, w_ukv):
    c_q = w_in[:, 0:Q_LORA]
    c_kv = w_in[:, Q_LORA:Q_LORA + KV_LORA]
    o = Q_LORA + KV_LORA
    k_r = w_in[:, o:o + QK_ROPE]
    o += QK_ROPE
    sb_q = w_in[:, o:o + SB_W] * (SB_DIM ** -0.5)
    sb_k = w_in[:, o + SB_W:o + 2 * SB_W]
    sb_v = w_in[:, o + 2 * SB_W:o + 3 * SB_W]
    kr, kr_rot = _rope_cols(k_r)
    w1 = jnp.concatenate([c_q, c_kv, sb_q, sb_k, sb_v, kr, kr_rot], axis=1).astype(BF16)

    uq = w_uq.reshape(Q_LORA, MLA_HEADS, QK_NOPE + QK_ROPE)
    nope = uq[:, :, :QK_NOPE].reshape(Q_LORA, MLA_HEADS * QK_NOPE)
    ropes = [_rope_cols(uq[:, h, QK_NOPE:]) for h in range(MLA_HEADS)]
    wq = jnp.concatenate([nope] + [r[0] for r in ropes] + [r[1] for r in ropes], axis=1).astype(BF16)

    ukv = w_ukv.reshape(KV_LORA, MLA_HEADS, QK_NOPE + V_DIM)
    wkv = jnp.concatenate([ukv[:, :, :QK_NOPE].reshape(KV_LORA, -1), ukv[:, :, QK_NOPE:].reshape(KV_LORA, -1)],
                          axis=1).astype(BF16)
    return w1, wq, wkv


def kernel(x, meta_tokens, w_in, q_norm_g, w_uq, kv_norm_g, w_ukv, mla_out_g, sb_out_g, w_o, ln1_g, ln1_b,
           w_router, b_router, w_gate, b_gate, w_up, b_up, w_down, b_down, ln2_g, ln2_b):
    b, seq, d = x.shape
    t = b * seq
    tm = min(256, seq)
    tq_mla = min(512, seq)
    tq_sb = min(256, seq)
    tt = min(512, t)
    row = lambda v: v.reshape(1, -1)

    w1, wq, wkv = _layout_weights(w_in[0], w_uq[0], w_ukv[0])
    cos_x, sin_x = _rope_tables(N_META + jnp.arange(seq))
    cos_m, sin_m = _rope_tables(jnp.arange(META_ROWS))
    meta = jnp.zeros((1, META_ROWS, d), x.dtype).at[0, :N_META].set(meta_tokens.astype(x.dtype))
    qg, kvg = row(q_norm_g[0]), row(kv_norm_g[0])
    q_mla, k_mla, v_mla, sb_q, sb_k, sb_v = _project(x, w1, qg, kvg, wq, wkv, cos_x, sin_x, tm)
    _, km_meta, vm_meta, _, sk_meta, sv_meta = _project(meta, w1, qg, kvg, wq, wkv, cos_m, sin_m, META_ROWS)
    o_mla = _mla_attention(q_mla, k_mla, v_mla, km_meta, vm_meta, tq_mla)
    o_sb = _sb_attention(sb_q, sb_k, sb_v, sk_meta, sv_meta, tq_sb)

    h1, h1b, logits_t = _post(x.reshape(t, d), o_mla.reshape(t, MLA_W), o_sb.reshape(t, SB_W),
                              row(mla_out_g[0]), row(sb_out_g[0]), w_o[0].astype(BF16),
                              row(ln1_g[0]), row(ln1_b[0]), w_router[0].T, b_router[0].reshape(-1, 1), tm)

    top_i, gates, rank, cnt = _route(logits_t, tt)
    counts = cnt[:, 0].astype(I32)
    padded = (counts + ROW_BLOCK - 1) // ROW_BLOCK * ROW_BLOCK
    p_end = jnp.cumsum(padded)
    p_start = p_end - padded
    dest = p_start[top_i] + rank
    n_blk = (t * TOP_K + N_EXPERTS * (ROW_BLOCK - 1) + ROW_BLOCK - 1) // ROW_BLOCK
    blk_e = jnp.minimum(jnp.sum(jnp.arange(n_blk)[:, None] * ROW_BLOCK >= p_end[None, :], axis=1),
                        N_EXPERTS - 1).astype(I32)
    n_used = (p_end[-1:] // ROW_BLOCK).astype(I32)

    tok_ids = jnp.broadcast_to(jnp.arange(t, dtype=I32)[None, :], (TOP_K, t))
    src = jnp.zeros((n_blk * ROW_BLOCK,), I32).at[dest.reshape(-1)].set(tok_ids.reshape(-1))
    buf = h1b[src]

    y = _experts(blk_e, n_used, buf,
                 w_gate[0].astype(BF16), b_gate[0].reshape(N_EXPERTS, 1, D_FF),
                 w_up[0].astype(BF16), b_up[0].reshape(N_EXPERTS, 1, D_FF),
                 w_down[0].astype(BF16), b_down[0].reshape(N_EXPERTS, 1, d))

    out = _combine(h1, y[dest], gates.T, row(ln2_g[0]), row(ln2_b[0]), tm)
    return out.reshape(b, seq, d)
```

```python
import functools

import jax
import jax.numpy as jnp
from jax import lax
from jax.experimental import pallas as pl
from jax.experimental.pallas import tpu as pltpu

F32 = jnp.float32
BF16 = jnp.bfloat16
I32 = jnp.int32

D_MODEL = 1024
N_META = 16
MLA_HEADS = 4
QK_NOPE = 128
QK_ROPE = 64
V_DIM = 128
Q_LORA = 256
KV_LORA = 256
ROPE_THETA = 10000.0
SB_HEADS = 8
SB_DIM = 64
MLA_W = MLA_HEADS * V_DIM
SB_W = SB_HEADS * SB_DIM
N_EXPERTS = 32
TOP_K = 4
D_FF = 1024
SWIGLU_LIMIT = 7.0
SWIGLU_ALPHA = 1.702
DEPTH = 1
DN_ALPHA = (2 * DEPTH) ** 0.25
LN_EPS = 1e-5
RMS_EPS = 1e-6
NEG_INF = -1e30

LANES = 128
META_ROWS = 128
QK_PAD = 2 * LANES
ROW_BLOCK = 256
VMEM_LIMIT = 56 * 1024 * 1024
SB_DEAD = -105.0


def _cparams(sem):
    return pltpu.CompilerParams(dimension_semantics=sem, vmem_limit_bytes=VMEM_LIMIT)


def _rms(x, g):
    return x * lax.rsqrt(jnp.mean(x * x, axis=-1, keepdims=True) + RMS_EPS) * g


def _layer_norm(x, g, b):
    mu = jnp.mean(x, axis=-1, keepdims=True)
    xc = x - mu
    var = jnp.mean(xc * xc, axis=-1, keepdims=True)
    return xc * lax.rsqrt(var + LN_EPS) * g + b


def _dot(a, b):
    return jnp.dot(a, b, preferred_element_type=F32)


def _dot_nt(a, b):
    return lax.dot_general(a, b, (((1,), (1,)), ((), ())), preferred_element_type=F32)


def _proj_kernel(x_ref, w1_ref, wsv_ref, qg_ref, kvg_ref, wq_ref, wk_ref, wv_ref, cos_ref, sin_ref,
                 qm_ref, km_ref, vmt_ref, sq_ref, sk_ref, svt_ref, *, tm, tkm, tks):
    xb = x_ref[0].astype(BF16)
    p = _dot(xb, w1_ref[...])
    sq_ref[0] = p[:, 512:1024].astype(BF16)
    sk_ref[0] = p[:, 1024:1536].astype(BF16)
    svt = _dot_nt(wsv_ref[...], xb).astype(BF16)
    for j in range(tm // tks):
        svt_ref[0, j] = svt[:, j * tks:(j + 1) * tks]
    cos = cos_ref[...]
    sin = sin_ref[...]
    k_rope = (p[:, 1536:1664] * cos + p[:, 1664:1792] * sin).astype(BF16)
    nq = _rms(p[:, 0:256], qg_ref[...]).astype(BF16)
    nkv = _rms(p[:, 256:512], kvg_ref[...]).astype(BF16)
    qq = _dot(nq, wq_ref[...])
    kn = _dot(nkv, wk_ref[...])
    vmt = _dot_nt(wv_ref[...], nkv).astype(BF16)
    for j in range(tm // tkm):
        vmt_ref[0, j] = vmt[:, j * tkm:(j + 1) * tkm]
    scale = (QK_NOPE + QK_ROPE) ** -0.5
    for h in range(MLA_HEADS):
        lo = h * QK_PAD
        qm_ref[0, :, lo:lo + LANES] = (qq[:, h * LANES:(h + 1) * LANES] * scale).astype(BF16)
        q_rope = qq[:, 512 + h * LANES:512 + (h + 1) * LANES] * cos + qq[:, 1024 + h * LANES:1024 + (h + 1) * LANES] * sin
        qm_ref[0, :, lo + LANES:lo + QK_PAD] = (q_rope * scale).astype(BF16)
        km_ref[0, :, lo:lo + LANES] = kn[:, h * LANES:(h + 1) * LANES].astype(BF16)
        km_ref[0, :, lo + LANES:lo + QK_PAD] = k_rope


def _project(x, w, cos, sin, tm, tkm, tks):
    b, l, d = x.shape
    full = lambda a: pl.BlockSpec(a.shape, lambda bi, i: (0,) * a.ndim)
    tok = lambda width: pl.BlockSpec((1, tm, width), lambda bi, i: (bi, i, 0))
    tab = pl.BlockSpec((tm, LANES), lambda bi, i: (i, 0))
    vt = lambda width, tk: pl.BlockSpec((1, tm // tk, width, tk), lambda bi, i: (bi, i, 0, 0))
    qk_w = MLA_HEADS * QK_PAD
    return pl.pallas_call(
        functools.partial(_proj_kernel, tm=tm, tkm=tkm, tks=tks),
        grid=(b, l // tm),
        in_specs=[tok(d)] + [full(a) for a in w] + [tab, tab],
        out_specs=[tok(qk_w), tok(qk_w), vt(MLA_W, tkm), tok(SB_W), tok(SB_W), vt(SB_W, tks)],
        out_shape=[jax.ShapeDtypeStruct((b, l, qk_w), BF16), jax.ShapeDtypeStruct((b, l, qk_w), BF16),
                   jax.ShapeDtypeStruct((b, l // tkm, MLA_W, tkm), BF16),
                   jax.ShapeDtypeStruct((b, l, SB_W), BF16), jax.ShapeDtypeStruct((b, l, SB_W), BF16),
                   jax.ShapeDtypeStruct((b, l // tks, SB_W, tks), BF16)],
        compiler_params=_cparams(("parallel", "parallel")),
        name="proj",
    )(x, *w, cos, sin)


def _mla_kernel(q_ref, k_ref, vt_ref, kmeta_ref, vtmeta_ref, o_ref, m_sc, l_sc, acc_sc, *, tq, tk):
    qi = pl.program_id(2)
    q = q_ref[0]

    s = _dot_nt(kmeta_ref[0], q)
    key = lax.broadcasted_iota(I32, s.shape, 0)
    s = jnp.where(key < N_META, s, NEG_INF)
    m0 = jnp.max(s, axis=0, keepdims=True)
    p0 = jnp.exp(s - m0)
    m_sc[...] = m0
    l_sc[...] = jnp.sum(p0, axis=0, keepdims=True)
    acc_sc[...] = _dot(vtmeta_ref[0, 0], p0.astype(BF16))

    def update(s, vt):
        m_prev = m_sc[...]
        m_new = jnp.maximum(m_prev, jnp.max(s, axis=0, keepdims=True))
        alpha = jnp.exp(m_prev - m_new)
        p = jnp.exp(s - m_new)
        l_sc[...] = alpha * l_sc[...] + jnp.sum(p, axis=0, keepdims=True)
        acc_sc[...] = alpha * acc_sc[...] + _dot(vt, p.astype(BF16))
        m_sc[...] = m_new

    def scores(ki):
        start = pl.multiple_of(ki * tk, tk)
        return _dot_nt(k_ref[0, pl.ds(start, tk), :], q)

    def body(ki, s):
        s_next = scores(ki + 1)
        update(s, vt_ref[0, ki])
        return s_next

    n_diag = tq // tk
    first_diag = qi * n_diag
    s = lax.fori_loop(0, first_diag, body, scores(0))
    key = lax.broadcasted_iota(I32, s.shape, 0)
    qry = lax.broadcasted_iota(I32, s.shape, 1)
    for j in range(n_diag):
        s_next = scores(first_diag + j + 1) if j + 1 < n_diag else None
        update(jnp.where(key + j * tk <= qry, s, NEG_INF), vt_ref[0, first_diag + j])
        s = s_next
    o_ref[0] = (acc_sc[...] / l_sc[...]).T


def _mla_attention(q, k, vt, kmeta, vtmeta, tq, tk):
    b, l, _ = q.shape
    return pl.pallas_call(
        functools.partial(_mla_kernel, tq=tq, tk=tk),
        grid=(b, MLA_HEADS, l // tq),
        in_specs=[
            pl.BlockSpec((1, tq, QK_PAD), lambda bi, h, i: (bi, i, h)),
            pl.BlockSpec((1, l, QK_PAD), lambda bi, h, i: (bi, 0, h)),
            pl.BlockSpec((1, l // tk, V_DIM, tk), lambda bi, h, i: (bi, 0, h, 0)),
            pl.BlockSpec((1, META_ROWS, QK_PAD), lambda bi, h, i: (0, 0, h)),
            pl.BlockSpec((1, 1, V_DIM, META_ROWS), lambda bi, h, i: (0, 0, h, 0)),
        ],
        out_specs=pl.BlockSpec((1, tq, V_DIM), lambda bi, h, i: (bi, i, h)),
        out_shape=jax.ShapeDtypeStruct((b, l, MLA_W), F32),
        scratch_shapes=[pltpu.VMEM((1, tq), F32), pltpu.VMEM((1, tq), F32), pltpu.VMEM((V_DIM, tq), F32)],
        compiler_params=_cparams(("parallel", "parallel", "arbitrary")),
        name="mla",
    )(q, k, vt, kmeta, vtmeta)


def _sb_kernel(q_ref, k_ref, vt_ref, kmeta_ref, vtmeta_ref, o_ref, c0_sc, c1_sc, a0_sc, a1_sc, *, tq):
    qi = pl.program_id(2)
    q = q_ref[0]
    lane = lax.broadcasted_iota(I32, q.shape, 1)
    zero = jnp.zeros_like(q)
    q_heads = (jnp.where(lane < SB_DIM, q, zero), jnp.where(lane >= SB_DIM, q, zero))
    c_refs = (c0_sc, c1_sc)
    acc_refs = (a0_sc, a1_sc)

    def later_matrix(n):
        r = lax.broadcasted_iota(I32, (n, n), 0)
        c = lax.broadcasted_iota(I32, (n, n), 1)
        return jnp.where(c > r, 1.0, 0.0).astype(BF16)

    for h in range(2):
        c_refs[h][...] = jnp.zeros_like(c_refs[h])
        acc_refs[h][...] = jnp.zeros_like(acc_refs[h])

    def block(kblk, vt, valid, u):
        heads = range(2)
        z = [_dot_nt(kblk, q_heads[h]) for h in heads]
        log_beta, log_keep = [], []
        for h in heads:
            softplus = jnp.maximum(z[h], 0.0) + jnp.log(1.0 + jnp.exp(-jnp.abs(z[h])))
            lk = -softplus
            if valid is not None:
                lk = jnp.where(valid, lk, 0.0)
            log_keep.append(lk)
            log_beta.append(z[h] - softplus)
        later = []
        for h in heads:
            hi = log_keep[h].astype(BF16)
            lo = (log_keep[h] - hi.astype(F32)).astype(BF16)
            later.append(_dot(u, hi) + _dot(u, lo) + c_refs[h][...])
        for h in heads:
            w = jnp.exp(log_beta[h] + later[h])
            if valid is not None:
                w = jnp.where(valid, w, 0.0)
            acc_refs[h][...] += _dot(vt, w.astype(BF16))
            c_refs[h][...] = later[h][0:1, :] + log_keep[h][0:1, :]

    def c_max():
        return jnp.maximum(jnp.max(c0_sc[...]), jnp.max(c1_sc[...]))

    u_blk = later_matrix(tq)
    start = pl.multiple_of(qi * tq, tq)
    key = lax.broadcasted_iota(I32, (tq, tq), 0)
    qry = lax.broadcasted_iota(I32, (tq, tq), 1)
    block(k_ref[0, pl.ds(start, tq), :], vt_ref[0, qi], key < qry, u_blk)

    def cond(carry):
        i, cm = carry
        return jnp.logical_and(i < qi, cm > SB_DEAD)

    def body(carry):
        i, _ = carry
        ki = qi - 1 - i
        st = pl.multiple_of(ki * tq, tq)
        block(k_ref[0, pl.ds(st, tq), :], vt_ref[0, ki], None, u_blk)
        return i + 1, c_max()

    _, cm = lax.while_loop(cond, body, (jnp.int32(0), c_max()))

    @pl.when(cm > SB_DEAD)
    def _():
        mkey = lax.broadcasted_iota(I32, (META_ROWS, tq), 0)
        block(kmeta_ref[0], vtmeta_ref[0, 0], mkey < N_META, later_matrix(META_ROWS))

    feat = lax.broadcasted_iota(I32, (LANES, tq), 0)
    o_ref[0] = jnp.where(feat < SB_DIM, a0_sc[...], a1_sc[...]).T


def _sb_attention(q, k, vt, kmeta, vtmeta, tq):
    b, l, _ = q.shape
    pairs = SB_W // LANES
    return pl.pallas_call(
        functools.partial(_sb_kernel, tq=tq),
        grid=(b, pairs, l // tq),
        in_specs=[
            pl.BlockSpec((1, tq, LANES), lambda bi, h, i: (bi, i, h)),
            pl.BlockSpec((1, l, LANES), lambda bi, h, i: (bi, 0, h)),
            pl.BlockSpec((1, l // tq, LANES, tq), lambda bi, h, i: (bi, 0, h, 0)),
            pl.BlockSpec((1, META_ROWS, LANES), lambda bi, h, i: (0, 0, h)),
            pl.BlockSpec((1, 1, LANES, META_ROWS), lambda bi, h, i: (0, 0, h, 0)),
        ],
        out_specs=pl.BlockSpec((1, tq, LANES), lambda bi, h, i: (bi, i, h)),
        out_shape=jax.ShapeDtypeStruct((b, l, SB_W), F32),
        scratch_shapes=[pltpu.VMEM((1, tq), F32), pltpu.VMEM((1, tq), F32),
                        pltpu.VMEM((LANES, tq), F32), pltpu.VMEM((LANES, tq), F32)],
        compiler_params=_cparams(("parallel", "parallel", "arbitrary")),
        name="sb",
    )(q, k, vt, kmeta, vtmeta)


def _post_kernel(x_ref, om_ref, os_ref, g1_ref, g2_ref, wo_ref, lg_ref, lb_ref, wr_ref, br_ref,
                 h1_ref, h1b_ref, lt_ref):
    n1 = _rms(om_ref[...], g1_ref[...]).astype(BF16)
    n2 = _rms(os_ref[...], g2_ref[...]).astype(BF16)
    a = _dot(n1, wo_ref[0:MLA_W, :]) + _dot(n2, wo_ref[MLA_W:MLA_W + SB_W, :])
    h1 = _layer_norm(DN_ALPHA * x_ref[...] + a, lg_ref[...], lb_ref[...])
    h1_ref[...] = h1
    h1b_ref[...] = h1.astype(BF16)
    lt_ref[...] = lax.dot_general(wr_ref[...], h1, (((1,), (1,)), ((), ())),
                                  precision=lax.Precision.HIGHEST, preferred_element_type=F32) + br_ref[...]


def _post(x2, om, osb, g1, g2, wo, lg, lb, wr_t, br, tm):
    t, d = x2.shape
    full = lambda a: pl.BlockSpec(a.shape, lambda i: (0,) * a.ndim)
    tok = lambda w: pl.BlockSpec((tm, w), lambda i: (i, 0))
    return pl.pallas_call(
        _post_kernel,
        grid=(t // tm,),
        in_specs=[tok(d), tok(MLA_W), tok(SB_W), full(g1), full(g2), full(wo), full(lg), full(lb), full(wr_t), full(br)],
        out_specs=[tok(d), tok(d), pl.BlockSpec((N_EXPERTS, tm), lambda i: (0, i))],
        out_shape=[jax.ShapeDtypeStruct((t, d), F32), jax.ShapeDtypeStruct((t, d), BF16),
                   jax.ShapeDtypeStruct((N_EXPERTS, t), F32)],
        compiler_params=_cparams(("parallel",)),
        name="post",
    )(x2, om, osb, g1, g2, wo, lg, lb, wr_t, br)


def _route_kernel(lt_ref, ti_ref, gate_ref, rank_ref, cnt_ref, carry_sc, *, tt):
    @pl.when(pl.program_id(0) == 0)
    def _():
        carry_sc[...] = jnp.zeros_like(carry_sc)

    logits = lt_ref[...]
    eidx = lax.broadcasted_iota(I32, logits.shape, 0)
    sels, vals = [], []
    for k in range(TOP_K):
        mx = jnp.max(logits, axis=0, keepdims=True)
        idx = jnp.min(jnp.where(logits == mx, eidx, N_EXPERTS), axis=0, keepdims=True)
        sel = eidx == idx
        logits = jnp.where(sel, -jnp.inf, logits)
        ti_ref[k:k + 1, :] = idx
        sels.append(sel)
        vals.append(mx)
    exps = [jnp.exp(v - vals[0]) for v in vals]
    denom = exps[0] + exps[1] + exps[2] + exps[3]
    for k in range(TOP_K):
        gate_ref[k:k + 1, :] = exps[k] / denom

    chosen = sum(jnp.where(s, 1.0, 0.0) for s in sels)
    r = lax.broadcasted_iota(I32, (tt, tt), 0)
    c = lax.broadcasted_iota(I32, (tt, tt), 1)
    before = jnp.where(r < c, 1.0, 0.0).astype(BF16)
    rank = _dot(chosen.astype(BF16), before) + carry_sc[...]
    for k in range(TOP_K):
        rank_ref[k:k + 1, :] = jnp.sum(jnp.where(sels[k], rank, 0.0), axis=0, keepdims=True).astype(I32)
    carry_sc[...] += jnp.sum(chosen, axis=1, keepdims=True)
    cnt_ref[...] = jnp.broadcast_to(carry_sc[...], cnt_ref.shape)


def _route(lt, tt):
    e, t = lt.shape
    tok = pl.BlockSpec((TOP_K, tt), lambda i: (0, i))
    return pl.pallas_call(
        functools.partial(_route_kernel, tt=tt),
        grid=(t // tt,),
        in_specs=[pl.BlockSpec((e, tt), lambda i: (0, i))],
        out_specs=[tok, tok, tok, pl.BlockSpec((e, LANES), lambda i: (0, 0))],
        out_shape=[jax.ShapeDtypeStruct((TOP_K, t), I32), jax.ShapeDtypeStruct((TOP_K, t), F32),
                   jax.ShapeDtypeStruct((TOP_K, t), I32), jax.ShapeDtypeStruct((e, LANES), F32)],
        scratch_shapes=[pltpu.VMEM((e, 1), F32)],
        compiler_params=_cparams(("arbitrary",)),
        name="route",
    )(lt)


def _expert_kernel(be_ref, nu_ref, x_ref, wg_ref, bg_ref, wu_ref, bu_ref, wd_ref, bd_ref, y_ref):
    i = pl.program_id(0)

    @pl.when(i < nu_ref[0])
    def _():
        x = x_ref[...]
        g = jnp.minimum(_dot(x, wg_ref[0]) + bg_ref[0], SWIGLU_LIMIT)
        u = jnp.clip(_dot(x, wu_ref[0]) + bu_ref[0], -SWIGLU_LIMIT, SWIGLU_LIMIT)
        a = (u + 1.0) * (g * (1.0 / (1.0 + jnp.exp(-SWIGLU_ALPHA * g))))
        y_ref[...] = (_dot(a.astype(BF16), wd_ref[0]) + bd_ref[0]).astype(y_ref.dtype)

    @pl.when(i >= nu_ref[0])
    def _():
        y_ref[...] = jnp.zeros_like(y_ref)


def _experts(blk_e, n_used, buf, wg, bg, wu, bu, wd, bd):
    n_rows, d = buf.shape
    n_blk = n_rows // ROW_BLOCK
    wspec = lambda a: pl.BlockSpec((1,) + a.shape[1:], lambda i, be, nu: (be[i], 0, 0))
    rows = pl.BlockSpec((ROW_BLOCK, d), lambda i, be, nu: (i, 0))
    return pl.pallas_call(
        _expert_kernel,
        grid_spec=pltpu.PrefetchScalarGridSpec(
            num_scalar_prefetch=2,
            grid=(n_blk,),
            in_specs=[rows, wspec(wg), wspec(bg), wspec(wu), wspec(bu), wspec(wd), wspec(bd)],
            out_specs=rows,
        ),
        out_shape=jax.ShapeDtypeStruct((n_rows, d), BF16),
        compiler_params=_cparams(("arbitrary",)),
        name="experts",
    )(blk_e, n_used, buf, wg, bg, wu, bu, wd, bd)


def _combine_kernel(h1_ref, y_ref, gate_ref, lg_ref, lb_ref, o_ref):
    gates = gate_ref[...]
    f = gates[:, 0:1] * y_ref[0].astype(F32)
    for k in range(1, TOP_K):
        f = f + gates[:, k:k + 1] * y_ref[k].astype(F32)
    o_ref[...] = _layer_norm(DN_ALPHA * h1_ref[...] + f, lg_ref[...], lb_ref[...])


def _combine(h1, yg, gates, lg, lb, tm):
    t, d = h1.shape
    full = lambda a: pl.BlockSpec(a.shape, lambda i: (0,) * a.ndim)
    return pl.pallas_call(
        _combine_kernel,
        grid=(t // tm,),
        in_specs=[pl.BlockSpec((tm, d), lambda i: (i, 0)), pl.BlockSpec((TOP_K, tm, d), lambda i: (0, i, 0)),
                  pl.BlockSpec((tm, TOP_K), lambda i: (i, 0)), full(lg), full(lb)],
        out_specs=pl.BlockSpec((tm, d), lambda i: (i, 0)),
        out_shape=jax.ShapeDtypeStruct((t, d), F32),
        compiler_params=_cparams(("parallel",)),
        name="combine",
    )(h1, yg, gates, lg, lb)


def _rope_tables(positions):
    half = QK_ROPE // 2
    freqs = ROPE_THETA ** (-jnp.arange(half, dtype=F32) * 2.0 / QK_ROPE)
    ang = positions.astype(F32)[:, None] * freqs[None, :]
    pad = jnp.zeros((positions.shape[0], LANES - QK_ROPE), F32)
    cos, sin = jnp.cos(ang), jnp.sin(ang)
    return jnp.concatenate([cos, cos, pad], axis=1), jnp.concatenate([sin, sin, pad], axis=1)


def _rope_cols(w):
    half = QK_ROPE // 2
    x1, x2 = w[:, :half], w[:, half:]
    pad = jnp.zeros((w.shape[0], LANES - QK_ROPE), w.dtype)
    return jnp.concatenate([x1, x2, pad], axis=1), jnp.concatenate([-x2, x1, pad], axis=1)


def _layout_weights(w_in, q_norm_g, w_uq, kv_norm_g, w_ukv):
    c_q = w_in[:, 0:Q_LORA]
    c_kv = w_in[:, Q_LORA:Q_LORA + KV_LORA]
    o = Q_LORA + KV_LORA
    k_r = w_in[:, o:o + QK_ROPE]
    o += QK_ROPE
    sb_q = w_in[:, o:o + SB_W] * (SB_DIM ** -0.5)
    sb_k = w_in[:, o + SB_W:o + 2 * SB_W]
    sb_v = w_in[:, o + 2 * SB_W:o + 3 * SB_W]
    kr, kr_rot = _rope_cols(k_r)
    w1 = jnp.concatenate([c_q, c_kv, sb_q, sb_k, kr, kr_rot], axis=1).astype(BF16)
    wsv_t = sb_v.T.astype(BF16)

    uq = w_uq.reshape(Q_LORA, MLA_HEADS, QK_NOPE + QK_ROPE)
    nope = uq[:, :, :QK_NOPE].reshape(Q_LORA, MLA_HEADS * QK_NOPE)
    ropes = [_rope_cols(uq[:, h, QK_NOPE:]) for h in range(MLA_HEADS)]
    wq = jnp.concatenate([nope] + [r[0] for r in ropes] + [r[1] for r in ropes], axis=1).astype(BF16)

    ukv = w_ukv.reshape(KV_LORA, MLA_HEADS, QK_NOPE + V_DIM)
    wk = ukv[:, :, :QK_NOPE].reshape(KV_LORA, -1).astype(BF16)
    wv_t = ukv[:, :, QK_NOPE:].reshape(KV_LORA, -1).T.astype(BF16)
    return (w1, wsv_t, q_norm_g.reshape(1, -1), kv_norm_g.reshape(1, -1), wq, wk, wv_t)


def kernel(x, meta_tokens, w_in, q_norm_g, w_uq, kv_norm_g, w_ukv, mla_out_g, sb_out_g, w_o, ln1_g, ln1_b,
           w_router, b_router, w_gate, b_gate, w_up, b_up, w_down, b_down, ln2_g, ln2_b):
    b, seq, d = x.shape
    t = b * seq
    tq_mla = min(512, seq)
    tk_mla = min(512, seq)
    tq_sb = min(256, seq)
    tm_proj = min(512, seq)
    tm = min(256, seq)
    tt = min(512, t)
    row = lambda v: v.reshape(1, -1)

    w = _layout_weights(w_in[0], q_norm_g[0], w_uq[0], kv_norm_g[0], w_ukv[0])
    cos_x, sin_x = _rope_tables(N_META + jnp.arange(seq))
    cos_m, sin_m = _rope_tables(jnp.arange(META_ROWS))
    meta = jnp.zeros((1, META_ROWS, d), x.dtype).at[0, :N_META].set(meta_tokens.astype(x.dtype))
    q_mla, k_mla, vt_mla, sb_q, sb_k, sb_vt = _project(x, w, cos_x, sin_x, tm_proj, tk_mla, tq_sb)
    _, km_meta, vtm_meta, _, sk_meta, svt_meta = _project(meta, w, cos_m, sin_m, META_ROWS, META_ROWS, META_ROWS)
    o_mla = _mla_attention(q_mla, k_mla, vt_mla, km_meta, vtm_meta, tq_mla, tk_mla)
    o_sb = _sb_attention(sb_q, sb_k, sb_vt, sk_meta, svt_meta, tq_sb)

    h1, h1b, logits_t = _post(x.reshape(t, d), o_mla.reshape(t, MLA_W), o_sb.reshape(t, SB_W),
                              row(mla_out_g[0]), row(sb_out_g[0]), w_o[0].astype(BF16),
                              row(ln1_g[0]), row(ln1_b[0]), w_router[0].T, b_router[0].reshape(-1, 1), tm)

    top_i, gates, rank, cnt = _route(logits_t, tt)
    counts = cnt[:, 0].astype(I32)
    padded = (counts + ROW_BLOCK - 1) // ROW_BLOCK * ROW_BLOCK
    p_end = jnp.cumsum(padded)
    p_start = p_end - padded
    dest = p_start[top_i] + rank
    n_blk = (t * TOP_K + N_EXPERTS * (ROW_BLOCK - 1) + ROW_BLOCK - 1) // ROW_BLOCK
    blk_e = jnp.minimum(jnp.sum(jnp.arange(n_blk)[:, None] * ROW_BLOCK >= p_end[None, :], axis=1),
                        N_EXPERTS - 1).astype(I32)
    n_used = (p_end[-1:] // ROW_BLOCK).astype(I32)

    tok_ids = jnp.broadcast_to(jnp.arange(t, dtype=I32)[None, :], (TOP_K, t))
    src = jnp.zeros((n_blk * ROW_BLOCK,), I32).at[dest.reshape(-1)].set(tok_ids.reshape(-1))
    buf = h1b[src]

    y = _experts(blk_e, n_used, buf,
                 w_gate[0].astype(BF16), b_gate[0].reshape(N_EXPERTS, 1, D_FF),
                 w_up[0].astype(BF16), b_up[0].reshape(N_EXPERTS, 1, D_FF),
                 w_down[0].astype(BF16), b_down[0].reshape(N_EXPERTS, 1, d))

    out = _combine(h1, y[dest], gates.T, row(ln2_g[0]), row(ln2_b[0]), tm)
    return out.reshape(b, seq, d)
```

```python
import functools

import jax
import jax.numpy as jnp
from jax import lax
from jax.experimental import pallas as pl
from jax.experimental.pallas import tpu as pltpu

F32 = jnp.float32
BF16 = jnp.bfloat16
I32 = jnp.int32
U32 = jnp.uint32

D_MODEL = 1024
N_META = 16
MLA_HEADS = 4
QK_NOPE = 128
QK_ROPE = 64
V_DIM = 128
Q_LORA = 256
KV_LORA = 256
ROPE_THETA = 10000.0
SB_HEADS = 8
SB_DIM = 64
MLA_W = MLA_HEADS * V_DIM
SB_W = SB_HEADS * SB_DIM
N_EXPERTS = 32
TOP_K = 4
D_FF = 1024
SWIGLU_LIMIT = 7.0
SWIGLU_ALPHA = 1.702
DEPTH = 1
DN_ALPHA = (2 * DEPTH) ** 0.25
LN_EPS = 1e-5
RMS_EPS = 1e-6
NEG_INF = -1e30

LANES = 128
META_ROWS = 128
QK_PAD = 2 * LANES
ROW_BLOCK = 512
LOG2_E = 1.4426950408889634
VMEM_LIMIT = 56 * 1024 * 1024
SB_DEAD = -105.0


def _cparams(sem):
    return pltpu.CompilerParams(dimension_semantics=sem, vmem_limit_bytes=VMEM_LIMIT)


def _rms(x, g):
    return x * lax.rsqrt(jnp.mean(x * x, axis=-1, keepdims=True) + RMS_EPS) * g


def _layer_norm(x, g, b):
    mu = jnp.mean(x, axis=-1, keepdims=True)
    xc = x - mu
    var = jnp.mean(xc * xc, axis=-1, keepdims=True)
    return xc * lax.rsqrt(var + LN_EPS) * g + b


def _dot(a, b):
    return jnp.dot(a, b, preferred_element_type=F32)


def _dot_nt(a, b):
    return lax.dot_general(a, b, (((1,), (1,)), ((), ())), preferred_element_type=F32)


def _bf16_bits(x):
    bits = pltpu.bitcast(x, U32)
    return (bits + jnp.uint32(0x7FFF) + ((bits >> 16) & jnp.uint32(1))) & jnp.uint32(0xFFFF0000)


def _pack_rows(x):
    n = x.shape[1] // 2
    return _bf16_bits(x[:, n:]) | (_bf16_bits(x[:, :n]) >> 16)


def _unpack_rows(p):
    lo = pltpu.bitcast(p << 16, F32)
    hi = pltpu.bitcast(p & jnp.uint32(0xFFFF0000), F32)
    return lo.astype(BF16), hi.astype(BF16)


def _proj_kernel(x_ref, w1_ref, wsv_ref, qg_ref, kvg_ref, wq_ref, wk_ref, wv_ref, cos_ref, sin_ref,
                 qm_ref, km_ref, vmt_ref, sq_ref, sk_ref, svt_ref, *, tm, tkm, tks):
    xb = x_ref[0].astype(BF16)
    p = _dot(xb, w1_ref[...])
    sq_ref[0] = p[:, 512:1024].astype(BF16)
    sk_ref[0] = p[:, 1024:1536].astype(BF16)
    svt = _dot_nt(wsv_ref[...], xb).astype(BF16)
    for j in range(tm // tks):
        svt_ref[0, j] = svt[:, j * tks:(j + 1) * tks]
    cos = cos_ref[...]
    sin = sin_ref[...]
    k_rope = (p[:, 1536:1664] * cos + p[:, 1664:1792] * sin).astype(BF16)
    nq = _rms(p[:, 0:256], qg_ref[...]).astype(BF16)
    nkv = _rms(p[:, 256:512], kvg_ref[...]).astype(BF16)
    qq = _dot(nq, wq_ref[...])
    kn = _dot(nkv, wk_ref[...])
    vmt = _dot_nt(wv_ref[...], nkv).astype(BF16)
    for j in range(tm // tkm):
        vmt_ref[0, j] = vmt[:, j * tkm:(j + 1) * tkm]
    scale = (QK_NOPE + QK_ROPE) ** -0.5 * LOG2_E
    for h in range(MLA_HEADS):
        lo = h * QK_PAD
        qm_ref[0, :, lo:lo + LANES] = (qq[:, h * LANES:(h + 1) * LANES] * scale).astype(BF16)
        q_rope = qq[:, 512 + h * LANES:512 + (h + 1) * LANES] * cos + qq[:, 1024 + h * LANES:1024 + (h + 1) * LANES] * sin
        qm_ref[0, :, lo + LANES:lo + QK_PAD] = (q_rope * scale).astype(BF16)
        km_ref[0, :, lo:lo + LANES] = kn[:, h * LANES:(h + 1) * LANES].astype(BF16)
        km_ref[0, :, lo + LANES:lo + QK_PAD] = k_rope


def _project(x, w, cos, sin, tm, tkm, tks):
    b, l, d = x.shape
    full = lambda a: pl.BlockSpec(a.shape, lambda bi, i: (0,) * a.ndim)
    tok = lambda width: pl.BlockSpec((1, tm, width), lambda bi, i: (bi, i, 0))
    tab = pl.BlockSpec((tm, LANES), lambda bi, i: (i, 0))
    vt = lambda width, tk: pl.BlockSpec((1, tm // tk, width, tk), lambda bi, i: (bi, i, 0, 0))
    qk_w = MLA_HEADS * QK_PAD
    return pl.pallas_call(
        functools.partial(_proj_kernel, tm=tm, tkm=tkm, tks=tks),
        grid=(b, l // tm),
        in_specs=[tok(d)] + [full(a) for a in w] + [tab, tab],
        out_specs=[tok(qk_w), tok(qk_w), vt(MLA_W, tkm), tok(SB_W), tok(SB_W), vt(SB_W, tks)],
        out_shape=[jax.ShapeDtypeStruct((b, l, qk_w), BF16), jax.ShapeDtypeStruct((b, l, qk_w), BF16),
                   jax.ShapeDtypeStruct((b, l // tkm, MLA_W, tkm), BF16),
                   jax.ShapeDtypeStruct((b, l, SB_W), BF16), jax.ShapeDtypeStruct((b, l, SB_W), BF16),
                   jax.ShapeDtypeStruct((b, l // tks, SB_W, tks), BF16)],
        compiler_params=_cparams(("parallel", "parallel")),
        name="proj",
    )(x, *w, cos, sin)


def _mla_kernel(q_ref, k_ref, vt_ref, kmeta_ref, vtmeta_ref, o_ref, m_sc, l_sc, acc_sc, *, tq, tk):
    qi = pl.program_id(2)
    q = q_ref[0]

    s = _dot_nt(kmeta_ref[0], q)
    key = lax.broadcasted_iota(I32, s.shape, 0)
    s = jnp.where(key < N_META, s, NEG_INF)
    m0 = jnp.max(s, axis=0, keepdims=True)
    p0 = jnp.exp2(s - m0)
    m_sc[...] = m0
    l_sc[...] = jnp.sum(p0, axis=0, keepdims=True)
    acc_sc[...] = _dot(vtmeta_ref[0, 0], p0.astype(BF16))

    def update(s, vt):
        m_prev = m_sc[...]
        m_new = jnp.maximum(m_prev, jnp.max(s, axis=0, keepdims=True))
        alpha = jnp.exp2(m_prev - m_new)
        p = jnp.exp2(s - m_new)
        l_sc[...] = alpha * l_sc[...] + jnp.sum(p, axis=0, keepdims=True)
        acc_sc[...] = alpha * acc_sc[...] + _dot(vt, p.astype(BF16))
        m_sc[...] = m_new

    def scores(ki):
        start = pl.multiple_of(ki * tk, tk)
        return _dot_nt(k_ref[0, pl.ds(start, tk), :], q)

    def body(ki, s):
        s_next = scores(ki + 1)
        update(s, vt_ref[0, ki])
        return s_next

    n_diag = tq // tk
    first_diag = qi * n_diag
    s = lax.fori_loop(0, first_diag, body, scores(0))
    key = lax.broadcasted_iota(I32, s.shape, 0)
    qry = lax.broadcasted_iota(I32, s.shape, 1)
    for j in range(n_diag):
        s_next = scores(first_diag + j + 1) if j + 1 < n_diag else None
        update(jnp.where(key + j * tk <= qry, s, NEG_INF), vt_ref[0, first_diag + j])
        s = s_next
    o_ref[0] = (acc_sc[...] / l_sc[...]).T


def _mla_attention(q, k, vt, kmeta, vtmeta, tq, tk):
    b, l, _ = q.shape
    return pl.pallas_call(
        functools.partial(_mla_kernel, tq=tq, tk=tk),
        grid=(b, MLA_HEADS, l // tq),
        in_specs=[
            pl.BlockSpec((1, tq, QK_PAD), lambda bi, h, i: (bi, i, h)),
            pl.BlockSpec((1, l, QK_PAD), lambda bi, h, i: (bi, 0, h)),
            pl.BlockSpec((1, l // tk, V_DIM, tk), lambda bi, h, i: (bi, 0, h, 0)),
            pl.BlockSpec((1, META_ROWS, QK_PAD), lambda bi, h, i: (0, 0, h)),
            pl.BlockSpec((1, 1, V_DIM, META_ROWS), lambda bi, h, i: (0, 0, h, 0)),
        ],
        out_specs=pl.BlockSpec((1, tq, V_DIM), lambda bi, h, i: (bi, i, h)),
        out_shape=jax.ShapeDtypeStruct((b, l, MLA_W), F32),
        scratch_shapes=[pltpu.VMEM((1, tq), F32), pltpu.VMEM((1, tq), F32), pltpu.VMEM((V_DIM, tq), F32)],
        compiler_params=_cparams(("parallel", "parallel", "arbitrary")),
        name="mla",
    )(q, k, vt, kmeta, vtmeta)


def _sb_kernel(q_ref, k_ref, vt_ref, kmeta_ref, vtmeta_ref, o_ref, c0_sc, c1_sc, a0_sc, a1_sc, *, tq):
    qi = pl.program_id(2)
    q = q_ref[0]
    lane = lax.broadcasted_iota(I32, q.shape, 1)
    zero = jnp.zeros_like(q)
    q_heads = (jnp.where(lane < SB_DIM, q, zero), jnp.where(lane >= SB_DIM, q, zero))
    c_refs = (c0_sc, c1_sc)
    acc_refs = (a0_sc, a1_sc)

    def later_matrix(n):
        r = lax.broadcasted_iota(I32, (n, n), 0)
        c = lax.broadcasted_iota(I32, (n, n), 1)
        return jnp.where(c > r, 1.0, 0.0).astype(BF16)

    for h in range(2):
        c_refs[h][...] = jnp.zeros_like(c_refs[h])
        acc_refs[h][...] = jnp.zeros_like(acc_refs[h])

    def block(kblk, vt, valid, u):
        heads = range(2)
        z = [_dot_nt(kblk, q_heads[h]) for h in heads]
        log_beta, log_keep = [], []
        for h in heads:
            softplus = jnp.maximum(z[h], 0.0) + jnp.log(1.0 + jnp.exp(-jnp.abs(z[h])))
            lk = -softplus
            if valid is not None:
                lk = jnp.where(valid, lk, 0.0)
            log_keep.append(lk)
            log_beta.append(z[h] - softplus)
        later = []
        for h in heads:
            hi = log_keep[h].astype(BF16)
            lo = (log_keep[h] - hi.astype(F32)).astype(BF16)
            later.append(_dot(u, hi) + _dot(u, lo) + c_refs[h][...])
        for h in heads:
            w = jnp.exp(log_beta[h] + later[h])
            if valid is not None:
                w = jnp.where(valid, w, 0.0)
            acc_refs[h][...] += _dot(vt, w.astype(BF16))
            c_refs[h][...] = later[h][0:1, :] + log_keep[h][0:1, :]

    def c_max():
        return jnp.maximum(jnp.max(c0_sc[...]), jnp.max(c1_sc[...]))

    u_blk = later_matrix(tq)
    start = pl.multiple_of(qi * tq, tq)
    key = lax.broadcasted_iota(I32, (tq, tq), 0)
    qry = lax.broadcasted_iota(I32, (tq, tq), 1)
    block(k_ref[0, pl.ds(start, tq), :], vt_ref[0, qi], key < qry, u_blk)

    def cond(carry):
        i, cm = carry
        return jnp.logical_and(i < qi, cm > SB_DEAD)

    def body(carry):
        i, _ = carry
        ki = qi - 1 - i
        st = pl.multiple_of(ki * tq, tq)
        block(k_ref[0, pl.ds(st, tq), :], vt_ref[0, ki], None, u_blk)
        return i + 1, c_max()

    _, cm = lax.while_loop(cond, body, (jnp.int32(0), c_max()))

    @pl.when(cm > SB_DEAD)
    def _():
        mkey = lax.broadcasted_iota(I32, (META_ROWS, tq), 0)
        block(kmeta_ref[0], vtmeta_ref[0, 0], mkey < N_META, later_matrix(META_ROWS))

    feat = lax.broadcasted_iota(I32, (LANES, tq), 0)
    o_ref[0] = jnp.where(feat < SB_DIM, a0_sc[...], a1_sc[...]).T


def _sb_attention(q, k, vt, kmeta, vtmeta, tq):
    b, l, _ = q.shape
    pairs = SB_W // LANES
    return pl.pallas_call(
        functools.partial(_sb_kernel, tq=tq),
        grid=(b, pairs, l // tq),
        in_specs=[
            pl.BlockSpec((1, tq, LANES), lambda bi, h, i: (bi, i, h)),
            pl.BlockSpec((1, l, LANES), lambda bi, h, i: (bi, 0, h)),
            pl.BlockSpec((1, l // tq, LANES, tq), lambda bi, h, i: (bi, 0, h, 0)),
            pl.BlockSpec((1, META_ROWS, LANES), lambda bi, h, i: (0, 0, h)),
            pl.BlockSpec((1, 1, LANES, META_ROWS), lambda bi, h, i: (0, 0, h, 0)),
        ],
        out_specs=pl.BlockSpec((1, tq, LANES), lambda bi, h, i: (bi, i, h)),
        out_shape=jax.ShapeDtypeStruct((b, l, SB_W), F32),
        scratch_shapes=[pltpu.VMEM((1, tq), F32), pltpu.VMEM((1, tq), F32),
                        pltpu.VMEM((LANES, tq), F32), pltpu.VMEM((LANES, tq), F32)],
        compiler_params=_cparams(("parallel", "parallel", "arbitrary")),
        name="sb",
    )(q, k, vt, kmeta, vtmeta)


def _post_kernel(x_ref, om_ref, os_ref, g1_ref, g2_ref, wo_ref, lg_ref, lb_ref, wr_ref, br_ref,
                 h1_ref, h1p_ref, lt_ref):
    n1 = _rms(om_ref[...], g1_ref[...]).astype(BF16)
    n2 = _rms(os_ref[...], g2_ref[...]).astype(BF16)
    a = _dot(n1, wo_ref[0:MLA_W, :]) + _dot(n2, wo_ref[MLA_W:MLA_W + SB_W, :])
    h1 = _layer_norm(DN_ALPHA * x_ref[...] + a, lg_ref[...], lb_ref[...])
    h1_ref[...] = h1
    h1p_ref[...] = _pack_rows(h1)
    lt_ref[...] = lax.dot_general(wr_ref[...], h1, (((1,), (1,)), ((), ())),
                                  precision=lax.Precision.HIGHEST, preferred_element_type=F32) + br_ref[...]


def _post(x2, om, osb, g1, g2, wo, lg, lb, wr_t, br, tm):
    t, d = x2.shape
    full = lambda a: pl.BlockSpec(a.shape, lambda i: (0,) * a.ndim)
    tok = lambda w: pl.BlockSpec((tm, w), lambda i: (i, 0))
    return pl.pallas_call(
        _post_kernel,
        grid=(t // tm,),
        in_specs=[tok(d), tok(MLA_W), tok(SB_W), full(g1), full(g2), full(wo), full(lg), full(lb), full(wr_t), full(br)],
        out_specs=[tok(d), tok(d // 2), pl.BlockSpec((N_EXPERTS, tm), lambda i: (0, i))],
        out_shape=[jax.ShapeDtypeStruct((t, d), F32), jax.ShapeDtypeStruct((t, d // 2), U32),
                   jax.ShapeDtypeStruct((N_EXPERTS, t), F32)],
        compiler_params=_cparams(("parallel",)),
        name="post",
    )(x2, om, osb, g1, g2, wo, lg, lb, wr_t, br)


def _route_kernel(lt_ref, ti_ref, gate_ref, rank_ref, cnt_ref, carry_sc, *, tt):
    @pl.when(pl.program_id(0) == 0)
    def _():
        carry_sc[...] = jnp.zeros_like(carry_sc)

    logits = lt_ref[...]
    eidx = lax.broadcasted_iota(I32, logits.shape, 0)
    sels, vals = [], []
    for k in range(TOP_K):
        mx = jnp.max(logits, axis=0, keepdims=True)
        idx = jnp.min(jnp.where(logits == mx, eidx, N_EXPERTS), axis=0, keepdims=True)
        sel = eidx == idx
        logits = jnp.where(sel, -jnp.inf, logits)
        ti_ref[k:k + 1, :] = idx
        sels.append(sel)
        vals.append(mx)
    exps = [jnp.exp(v - vals[0]) for v in vals]
    denom = exps[0] + exps[1] + exps[2] + exps[3]
    for k in range(TOP_K):
        gate_ref[k:k + 1, :] = exps[k] / denom

    chosen = sum(jnp.where(s, 1.0, 0.0) for s in sels)
    r = lax.broadcasted_iota(I32, (tt, tt), 0)
    c = lax.broadcasted_iota(I32, (tt, tt), 1)
    before = jnp.where(r < c, 1.0, 0.0).astype(BF16)
    rank = _dot(chosen.astype(BF16), before) + carry_sc[...]
    for k in range(TOP_K):
        rank_ref[k:k + 1, :] = jnp.sum(jnp.where(sels[k], rank, 0.0), axis=0, keepdims=True).astype(I32)
    carry_sc[...] += jnp.sum(chosen, axis=1, keepdims=True)
    cnt_ref[...] = jnp.broadcast_to(carry_sc[...], cnt_ref.shape)


def _route(lt, tt):
    e, t = lt.shape
    tok = pl.BlockSpec((TOP_K, tt), lambda i: (0, i))
    return pl.pallas_call(
        functools.partial(_route_kernel, tt=tt),
        grid=(t // tt,),
        in_specs=[pl.BlockSpec((e, tt), lambda i: (0, i))],
        out_specs=[tok, tok, tok, pl.BlockSpec((e, LANES), lambda i: (0, 0))],
        out_shape=[jax.ShapeDtypeStruct((TOP_K, t), I32), jax.ShapeDtypeStruct((TOP_K, t), F32),
                   jax.ShapeDtypeStruct((TOP_K, t), I32), jax.ShapeDtypeStruct((e, LANES), F32)],
        scratch_shapes=[pltpu.VMEM((e, 1), F32)],
        compiler_params=_cparams(("arbitrary",)),
        name="route",
    )(lt)


def _row_copy(src_hbm, src_row, dst_hbm, dst_row, sem):
    return pltpu.make_async_copy(src_hbm.at[pl.ds(src_row, 1)], dst_hbm.at[pl.ds(dst_row, 1)], sem)


def _all_rows_wait(dst_hbm, n, sem):
    pltpu.make_async_copy(dst_hbm.at[pl.ds(0, n)], dst_hbm.at[pl.ds(0, n)], sem).wait()


def _scatter_kernel(dest_ref, h_hbm, buf_in_hbm, buf_hbm, sem, *, tg):
    del buf_in_hbm
    base = pl.program_id(0) * tg

    def body(j, carry):
        for k in range(TOP_K):
            _row_copy(h_hbm, base + j, buf_hbm, dest_ref[k, j], sem).start()
        return carry

    lax.fori_loop(0, tg, body, 0)
    _all_rows_wait(buf_hbm, TOP_K * tg, sem)


def _scatter_rows(dest, h1p, n_rows, tg):
    t, w = h1p.shape
    any_spec = pl.BlockSpec(memory_space=pl.ANY)
    return pl.pallas_call(
        functools.partial(_scatter_kernel, tg=tg),
        grid=(t // tg,),
        in_specs=[pl.BlockSpec((TOP_K, tg), lambda i: (0, i), memory_space=pltpu.SMEM), any_spec, any_spec],
        out_specs=any_spec,
        out_shape=jax.ShapeDtypeStruct((n_rows, w), U32),
        scratch_shapes=[pltpu.SemaphoreType.DMA],
        input_output_aliases={2: 0},
        compiler_params=_cparams(("arbitrary",)),
        name="scatter_rows",
    )(dest, h1p, jnp.zeros((n_rows, w), U32))


def _gather_kernel(dest_ref, y_hbm, yg_hbm, sem, *, tg, t):
    base = pl.program_id(0) * tg

    def body(j, carry):
        for k in range(TOP_K):
            _row_copy(y_hbm, dest_ref[k, j], yg_hbm, k * t + base + j, sem).start()
        return carry

    lax.fori_loop(0, tg, body, 0)
    _all_rows_wait(yg_hbm, TOP_K * tg, sem)


def _gather_rows(dest, yp, tg):
    _, t = dest.shape
    w = yp.shape[1]
    any_spec = pl.BlockSpec(memory_space=pl.ANY)
    return pl.pallas_call(
        functools.partial(_gather_kernel, tg=tg, t=t),
        grid=(t // tg,),
        in_specs=[pl.BlockSpec((TOP_K, tg), lambda i: (0, i), memory_space=pltpu.SMEM), any_spec],
        out_specs=any_spec,
        out_shape=jax.ShapeDtypeStruct((TOP_K * t, w), U32),
        scratch_shapes=[pltpu.SemaphoreType.DMA],
        compiler_params=_cparams(("arbitrary",)),
        name="gather_rows",
    )(dest, yp)


def _expert_kernel(be_ref, nu_ref, x_ref, wg_ref, bg_ref, wu_ref, bu_ref, wd_ref, bd_ref, y_ref,
                   wg_sc, wu_sc, wd_sc):
    i = pl.program_id(0)
    half = D_MODEL // 2
    active = i < nu_ref[0]

    @pl.when(jnp.logical_and(active, jnp.logical_or(i == 0, be_ref[i] != be_ref[jnp.maximum(i - 1, 0)])))
    def _():
        wg_sc[...] = wg_ref[0].astype(BF16)
        wu_sc[...] = wu_ref[0].astype(BF16)
        wd_sc[...] = wd_ref[0].astype(BF16)

    @pl.when(active)
    def _():
        x_lo, x_hi = _unpack_rows(x_ref[...])
        g = _dot(x_lo, wg_sc[0:half, :]) + _dot(x_hi, wg_sc[half:D_MODEL, :]) + bg_ref[0]
        u = _dot(x_lo, wu_sc[0:half, :]) + _dot(x_hi, wu_sc[half:D_MODEL, :]) + bu_ref[0]
        g = jnp.minimum(g, SWIGLU_LIMIT)
        u = jnp.clip(u, -SWIGLU_LIMIT, SWIGLU_LIMIT)
        a = (u + 1.0) * (g * (1.0 / (1.0 + jnp.exp(-SWIGLU_ALPHA * g))))
        y_ref[...] = _pack_rows(_dot(a.astype(BF16), wd_sc[...]) + bd_ref[0])

    @pl.when(i >= nu_ref[0])
    def _():
        y_ref[...] = jnp.zeros_like(y_ref)


def _experts(blk_e, n_used, buf, wg, bg, wu, bu, wd, bd):
    n_rows, d = buf.shape
    n_blk = n_rows // ROW_BLOCK
    wspec = lambda a: pl.BlockSpec((1,) + a.shape[1:], lambda i, be, nu: (be[i], 0, 0))
    rows = pl.BlockSpec((ROW_BLOCK, d), lambda i, be, nu: (i, 0))
    return pl.pallas_call(
        _expert_kernel,
        grid_spec=pltpu.PrefetchScalarGridSpec(
            num_scalar_prefetch=2,
            grid=(n_blk,),
            in_specs=[rows, wspec(wg), wspec(bg), wspec(wu), wspec(bu), wspec(wd), wspec(bd)],
            out_specs=rows,
            scratch_shapes=[pltpu.VMEM(wg.shape[1:], BF16), pltpu.VMEM(wu.shape[1:], BF16),
                            pltpu.VMEM(wd.shape[1:], BF16)],
        ),
        out_shape=jax.ShapeDtypeStruct((n_rows, d), U32),
        compiler_params=_cparams(("arbitrary",)),
        name="experts",
    )(blk_e, n_used, buf, wg, bg, wu, bu, wd, bd)


def _combine_kernel(h1_ref, y_ref, gate_ref, lg_ref, lb_ref, o_ref):
    gates = gate_ref[...]
    f_lo = f_hi = None
    for k in range(TOP_K):
        lo, hi = _unpack_rows(y_ref[k])
        g = gates[:, k:k + 1]
        f_lo = g * lo.astype(F32) if f_lo is None else f_lo + g * lo.astype(F32)
        f_hi = g * hi.astype(F32) if f_hi is None else f_hi + g * hi.astype(F32)
    f = jnp.concatenate([f_lo, f_hi], axis=1)
    o_ref[...] = _layer_norm(DN_ALPHA * h1_ref[...] + f, lg_ref[...], lb_ref[...])


def _combine(h1, yg, gates, lg, lb, tm):
    t, d = h1.shape
    full = lambda a: pl.BlockSpec(a.shape, lambda i: (0,) * a.ndim)
    return pl.pallas_call(
        _combine_kernel,
        grid=(t // tm,),
        in_specs=[pl.BlockSpec((tm, d), lambda i: (i, 0)), pl.BlockSpec((TOP_K, tm, d // 2), lambda i: (0, i, 0)),
                  pl.BlockSpec((tm, TOP_K), lambda i: (i, 0)), full(lg), full(lb)],
        out_specs=pl.BlockSpec((tm, d), lambda i: (i, 0)),
        out_shape=jax.ShapeDtypeStruct((t, d), F32),
        compiler_params=_cparams(("parallel",)),
        name="combine",
    )(h1, yg, gates, lg, lb)


def _rope_tables(positions):
    half = QK_ROPE // 2
    freqs = ROPE_THETA ** (-jnp.arange(half, dtype=F32) * 2.0 / QK_ROPE)
    ang = positions.astype(F32)[:, None] * freqs[None, :]
    pad = jnp.zeros((positions.shape[0], LANES - QK_ROPE), F32)
    cos, sin = jnp.cos(ang), jnp.sin(ang)
    return jnp.concatenate([cos, cos, pad], axis=1), jnp.concatenate([sin, sin, pad], axis=1)


def _rope_cols(w):
    half = QK_ROPE // 2
    x1, x2 = w[:, :half], w[:, half:]
    pad = jnp.zeros((w.shape[0], LANES - QK_ROPE), w.dtype)
    return jnp.concatenate([x1, x2, pad], axis=1), jnp.concatenate([-x2, x1, pad], axis=1)


def _layout_weights(w_in, q_norm_g, w_uq, kv_norm_g, w_ukv):
    c_q = w_in[:, 0:Q_LORA]
    c_kv = w_in[:, Q_LORA:Q_LORA + KV_LORA]
    o = Q_LORA + KV_LORA
    k_r = w_in[:, o:o + QK_ROPE]
    o += QK_ROPE
    sb_q = w_in[:, o:o + SB_W] * (SB_DIM ** -0.5)
    sb_k = w_in[:, o + SB_W:o + 2 * SB_W]
    sb_v = w_in[:, o + 2 * SB_W:o + 3 * SB_W]
    kr, kr_rot = _rope_cols(k_r)
    w1 = jnp.concatenate([c_q, c_kv, sb_q, sb_k, kr, kr_rot], axis=1).astype(BF16)
    wsv_t = sb_v.T.astype(BF16)

    uq = w_uq.reshape(Q_LORA, MLA_HEADS, QK_NOPE + QK_ROPE)
    nope = uq[:, :, :QK_NOPE].reshape(Q_LORA, MLA_HEADS * QK_NOPE)
    ropes = [_rope_cols(uq[:, h, QK_NOPE:]) for h in range(MLA_HEADS)]
    wq = jnp.concatenate([nope] + [r[0] for r in ropes] + [r[1] for r in ropes], axis=1).astype(BF16)

    ukv = w_ukv.reshape(KV_LORA, MLA_HEADS, QK_NOPE + V_DIM)
    wk = ukv[:, :, :QK_NOPE].reshape(KV_LORA, -1).astype(BF16)
    wv_t = ukv[:, :, QK_NOPE:].reshape(KV_LORA, -1).T.astype(BF16)
    return (w1, wsv_t, q_norm_g.reshape(1, -1), kv_norm_g.reshape(1, -1), wq, wk, wv_t)


def kernel(x, meta_tokens, w_in, q_norm_g, w_uq, kv_norm_g, w_ukv, mla_out_g, sb_out_g, w_o, ln1_g, ln1_b,
           w_router, b_router, w_gate, b_gate, w_up, b_up, w_down, b_down, ln2_g, ln2_b):
    b, seq, d = x.shape
    t = b * seq
    tq_mla = min(512, seq)
    tk_mla = min(512, seq)
    tq_sb = min(256, seq)
    tm_proj = min(512, seq)
    tm = min(256, seq)
    tt = min(512, t)
    tg = min(1024, t)
    row = lambda v: v.reshape(1, -1)

    w = _layout_weights(w_in[0], q_norm_g[0], w_uq[0], kv_norm_g[0], w_ukv[0])
    cos_x, sin_x = _rope_tables(N_META + jnp.arange(seq))
    cos_m, sin_m = _rope_tables(jnp.arange(META_ROWS))
    meta = jnp.zeros((1, META_ROWS, d), x.dtype).at[0, :N_META].set(meta_tokens.astype(x.dtype))
    q_mla, k_mla, vt_mla, sb_q, sb_k, sb_vt = _project(x, w, cos_x, sin_x, tm_proj, tk_mla, tq_sb)
    _, km_meta, vtm_meta, _, sk_meta, svt_meta = _project(meta, w, cos_m, sin_m, META_ROWS, META_ROWS, META_ROWS)
    o_mla = _mla_attention(q_mla, k_mla, vt_mla, km_meta, vtm_meta, tq_mla, tk_mla)
    o_sb = _sb_attention(sb_q, sb_k, sb_vt, sk_meta, svt_meta, tq_sb)

    h1, h1p, logits_t = _post(x.reshape(t, d), o_mla.reshape(t, MLA_W), o_sb.reshape(t, SB_W),
                              row(mla_out_g[0]), row(sb_out_g[0]), w_o[0].astype(BF16),
                              row(ln1_g[0]), row(ln1_b[0]), w_router[0].T, b_router[0].reshape(-1, 1), tm)

    top_i, gates, rank, cnt = _route(logits_t, tt)
    counts = cnt[:, 0].astype(I32)
    padded = (counts + ROW_BLOCK - 1) // ROW_BLOCK * ROW_BLOCK
    p_end = jnp.cumsum(padded)
    p_start = p_end - padded
    dest = p_start[top_i] + rank
    n_blk = (t * TOP_K + N_EXPERTS * (ROW_BLOCK - 1) + ROW_BLOCK - 1) // ROW_BLOCK
    blk_e = jnp.minimum(jnp.sum(jnp.arange(n_blk)[:, None] * ROW_BLOCK >= p_end[None, :], axis=1),
                        N_EXPERTS - 1).astype(I32)
    n_used = (p_end[-1:] // ROW_BLOCK).astype(I32)

    bufp = _scatter_rows(dest, h1p, n_blk * ROW_BLOCK, tg)
    yp = _experts(blk_e, n_used, bufp,
                  w_gate[0], b_gate[0].reshape(N_EXPERTS, 1, D_FF),
                  w_up[0], b_up[0].reshape(N_EXPERTS, 1, D_FF),
                  w_down[0], b_down[0].reshape(N_EXPERTS, 1, d))
    ygp = _gather_rows(dest, yp, tg).reshape(TOP_K, t, d // 2)

    out = _combine(h1, ygp, gates.T, row(ln2_g[0]), row(ln2_b[0]), tm)
    return out.reshape(b, seq, d)
```

```python
import functools

import jax
import jax.numpy as jnp
from jax import lax
from jax.experimental import pallas as pl
from jax.experimental.pallas import tpu as pltpu

F32 = jnp.float32
BF16 = jnp.bfloat16
I32 = jnp.int32
U32 = jnp.uint32

D_MODEL = 1024
N_META = 16
MLA_HEADS = 4
QK_NOPE = 128
QK_ROPE = 64
V_DIM = 128
Q_LORA = 256
KV_LORA = 256
ROPE_THETA = 10000.0
SB_HEADS = 8
SB_DIM = 64
MLA_W = MLA_HEADS * V_DIM
SB_W = SB_HEADS * SB_DIM
N_EXPERTS = 32
TOP_K = 4
D_FF = 1024
SWIGLU_LIMIT = 7.0
SWIGLU_ALPHA = 1.702
DEPTH = 1
DN_ALPHA = (2 * DEPTH) ** 0.25
LN_EPS = 1e-5
RMS_EPS = 1e-6
NEG_INF = -1e30

LANES = 128
META_ROWS = 128
QK_PAD = 2 * LANES
ROW_BLOCK = 512
LOG2_E = 1.4426950408889634
VMEM_LIMIT = 56 * 1024 * 1024
SB_DEAD = -105.0


def _cparams(sem):
    return pltpu.CompilerParams(dimension_semantics=sem, vmem_limit_bytes=VMEM_LIMIT)


def _rms(x, g):
    return x * lax.rsqrt(jnp.mean(x * x, axis=-1, keepdims=True) + RMS_EPS) * g


def _layer_norm(x, g, b):
    mu = jnp.mean(x, axis=-1, keepdims=True)
    xc = x - mu
    var = jnp.mean(xc * xc, axis=-1, keepdims=True)
    return xc * lax.rsqrt(var + LN_EPS) * g + b


def _dot(a, b):
    return jnp.dot(a, b, preferred_element_type=F32)


def _dot_nt(a, b):
    return lax.dot_general(a, b, (((1,), (1,)), ((), ())), preferred_element_type=F32)


def _bf16_bits(x):
    bits = pltpu.bitcast(x, U32)
    return (bits + jnp.uint32(0x7FFF) + ((bits >> 16) & jnp.uint32(1))) & jnp.uint32(0xFFFF0000)


def _pack_rows(x):
    n = x.shape[1] // 2
    return _bf16_bits(x[:, n:]) | (_bf16_bits(x[:, :n]) >> 16)


def _unpack_rows(p):
    lo = pltpu.bitcast(p << 16, F32)
    hi = pltpu.bitcast(p & jnp.uint32(0xFFFF0000), F32)
    return lo.astype(BF16), hi.astype(BF16)


def _proj_kernel(x_ref, w1_ref, wsv_ref, qg_ref, kvg_ref, wq_ref, wk_ref, wv_ref, cos_ref, sin_ref,
                 qm_ref, km_ref, vmt_ref, sq_ref, sk_ref, svt_ref, *, tm, tkm, tks):
    xb = x_ref[0].astype(BF16)
    p = _dot(xb, w1_ref[...])
    sq_ref[0] = p[:, 512:1024].astype(BF16)
    sk_ref[0] = p[:, 1024:1536].astype(BF16)
    svt = _dot_nt(wsv_ref[...], xb).astype(BF16)
    for j in range(tm // tks):
        svt_ref[0, j] = svt[:, j * tks:(j + 1) * tks]
    cos = cos_ref[...]
    sin = sin_ref[...]
    k_rope = (p[:, 1536:1664] * cos + p[:, 1664:1792] * sin).astype(BF16)
    nq = _rms(p[:, 0:256], qg_ref[...]).astype(BF16)
    nkv = _rms(p[:, 256:512], kvg_ref[...]).astype(BF16)
    qq = _dot(nq, wq_ref[...])
    kn = _dot(nkv, wk_ref[...])
    vmt = _dot_nt(wv_ref[...], nkv).astype(BF16)
    for j in range(tm // tkm):
        vmt_ref[0, j] = vmt[:, j * tkm:(j + 1) * tkm]
    scale = (QK_NOPE + QK_ROPE) ** -0.5 * LOG2_E
    for h in range(MLA_HEADS):
        lo = h * QK_PAD
        qm_ref[0, :, lo:lo + LANES] = (qq[:, h * LANES:(h + 1) * LANES] * scale).astype(BF16)
        q_rope = qq[:, 512 + h * LANES:512 + (h + 1) * LANES] * cos + qq[:, 1024 + h * LANES:1024 + (h + 1) * LANES] * sin
        qm_ref[0, :, lo + LANES:lo + QK_PAD] = (q_rope * scale).astype(BF16)
        km_ref[0, :, lo:lo + LANES] = kn[:, h * LANES:(h + 1) * LANES].astype(BF16)
        km_ref[0, :, lo + LANES:lo + QK_PAD] = k_rope


def _project(x, w, cos, sin, tm, tkm, tks):
    b, l, d = x.shape
    full = lambda a: pl.BlockSpec(a.shape, lambda bi, i: (0,) * a.ndim)
    tok = lambda width: pl.BlockSpec((1, tm, width), lambda bi, i: (bi, i, 0))
    tab = pl.BlockSpec((tm, LANES), lambda bi, i: (i, 0))
    vt = lambda width, tk: pl.BlockSpec((1, tm // tk, width, tk), lambda bi, i: (bi, i, 0, 0))
    qk_w = MLA_HEADS * QK_PAD
    return pl.pallas_call(
        functools.partial(_proj_kernel, tm=tm, tkm=tkm, tks=tks),
        grid=(b, l // tm),
        in_specs=[tok(d)] + [full(a) for a in w] + [tab, tab],
        out_specs=[tok(qk_w), tok(qk_w), vt(MLA_W, tkm), tok(SB_W), tok(SB_W), vt(SB_W, tks)],
        out_shape=[jax.ShapeDtypeStruct((b, l, qk_w), BF16), jax.ShapeDtypeStruct((b, l, qk_w), BF16),
                   jax.ShapeDtypeStruct((b, l // tkm, MLA_W, tkm), BF16),
                   jax.ShapeDtypeStruct((b, l, SB_W), BF16), jax.ShapeDtypeStruct((b, l, SB_W), BF16),
                   jax.ShapeDtypeStruct((b, l // tks, SB_W, tks), BF16)],
        compiler_params=_cparams(("parallel", "parallel")),
        name="proj",
    )(x, *w, cos, sin)


def _mla_kernel(q_ref, k_ref, vt_ref, kmeta_ref, vtmeta_ref, o_ref, m_sc, l_sc, acc_sc, *, tq, tk):
    qi = pl.program_id(2)
    q = q_ref[0]

    s = _dot_nt(kmeta_ref[0], q)
    key = lax.broadcasted_iota(I32, s.shape, 0)
    s = jnp.where(key < N_META, s, NEG_INF)
    m0 = jnp.max(s, axis=0, keepdims=True)
    p0 = jnp.exp2(s - m0)
    m_sc[...] = m0
    l_sc[...] = jnp.sum(p0, axis=0, keepdims=True)
    acc_sc[...] = _dot(vtmeta_ref[0, 0], p0.astype(BF16))

    def update(s, vt):
        m_prev = m_sc[...]
        m_new = jnp.maximum(m_prev, jnp.max(s, axis=0, keepdims=True))
        alpha = jnp.exp2(m_prev - m_new)
        p = jnp.exp2(s - m_new)
        l_sc[...] = alpha * l_sc[...] + jnp.sum(p, axis=0, keepdims=True)
        acc_sc[...] = alpha * acc_sc[...] + _dot(vt, p.astype(BF16))
        m_sc[...] = m_new

    def scores(ki):
        start = pl.multiple_of(ki * tk, tk)
        return _dot_nt(k_ref[0, pl.ds(start, tk), :], q)

    def body(ki, s):
        s_next = scores(ki + 1)
        update(s, vt_ref[0, ki])
        return s_next

    n_diag = tq // tk
    first_diag = qi * n_diag
    s = lax.fori_loop(0, first_diag, body, scores(0))
    key = lax.broadcasted_iota(I32, s.shape, 0)
    qry = lax.broadcasted_iota(I32, s.shape, 1)
    for j in range(n_diag):
        s_next = scores(first_diag + j + 1) if j + 1 < n_diag else None
        update(jnp.where(key + j * tk <= qry, s, NEG_INF), vt_ref[0, first_diag + j])
        s = s_next
    o_ref[0] = (acc_sc[...] / l_sc[...]).T


def _mla_attention(q, k, vt, kmeta, vtmeta, tq, tk):
    b, l, _ = q.shape
    return pl.pallas_call(
        functools.partial(_mla_kernel, tq=tq, tk=tk),
        grid=(b, MLA_HEADS, l // tq),
        in_specs=[
            pl.BlockSpec((1, tq, QK_PAD), lambda bi, h, i: (bi, i, h)),
            pl.BlockSpec((1, l, QK_PAD), lambda bi, h, i: (bi, 0, h)),
            pl.BlockSpec((1, l // tk, V_DIM, tk), lambda bi, h, i: (bi, 0, h, 0)),
            pl.BlockSpec((1, META_ROWS, QK_PAD), lambda bi, h, i: (0, 0, h)),
            pl.BlockSpec((1, 1, V_DIM, META_ROWS), lambda bi, h, i: (0, 0, h, 0)),
        ],
        out_specs=pl.BlockSpec((1, tq, V_DIM), lambda bi, h, i: (bi, i, h)),
        out_shape=jax.ShapeDtypeStruct((b, l, MLA_W), F32),
        scratch_shapes=[pltpu.VMEM((1, tq), F32), pltpu.VMEM((1, tq), F32), pltpu.VMEM((V_DIM, tq), F32)],
        compiler_params=_cparams(("parallel", "parallel", "arbitrary")),
        name="mla",
    )(q, k, vt, kmeta, vtmeta)


def _sb_kernel(q_ref, k_ref, vt_ref, kmeta_ref, vtmeta_ref, o_ref, c0_sc, c1_sc, a0_sc, a1_sc, *, tq):
    qi = pl.program_id(2)
    q = q_ref[0]
    lane = lax.broadcasted_iota(I32, q.shape, 1)
    zero = jnp.zeros_like(q)
    q_heads = (jnp.where(lane < SB_DIM, q, zero), jnp.where(lane >= SB_DIM, q, zero))
    c_refs = (c0_sc, c1_sc)
    acc_refs = (a0_sc, a1_sc)

    def later_matrix(n):
        r = lax.broadcasted_iota(I32, (n, n), 0)
        c = lax.broadcasted_iota(I32, (n, n), 1)
        return jnp.where(c > r, 1.0, 0.0).astype(BF16)

    for h in range(2):
        c_refs[h][...] = jnp.zeros_like(c_refs[h])
        acc_refs[h][...] = jnp.zeros_like(acc_refs[h])

    def block(kblk, vt, valid, u):
        heads = range(2)
        z = [_dot_nt(kblk, q_heads[h]) for h in heads]
        log_beta, log_keep = [], []
        for h in heads:
            softplus = jnp.maximum(z[h], 0.0) + jnp.log(1.0 + jnp.exp(-jnp.abs(z[h])))
            lk = -softplus
            if valid is not None:
                lk = jnp.where(valid, lk, 0.0)
            log_keep.append(lk)
            log_beta.append(z[h] - softplus)
        later = []
        for h in heads:
            hi = log_keep[h].astype(BF16)
            lo = (log_keep[h] - hi.astype(F32)).astype(BF16)
            later.append(_dot(u, hi) + _dot(u, lo) + c_refs[h][...])
        for h in heads:
            w = jnp.exp(log_beta[h] + later[h])
            if valid is not None:
                w = jnp.where(valid, w, 0.0)
            acc_refs[h][...] += _dot(vt, w.astype(BF16))
            c_refs[h][...] = later[h][0:1, :] + log_keep[h][0:1, :]

    def c_max():
        return jnp.maximum(jnp.max(c0_sc[...]), jnp.max(c1_sc[...]))

    u_blk = later_matrix(tq)
    start = pl.multiple_of(qi * tq, tq)
    key = lax.broadcasted_iota(I32, (tq, tq), 0)
    qry = lax.broadcasted_iota(I32, (tq, tq), 1)
    block(k_ref[0, pl.ds(start, tq), :], vt_ref[0, qi], key < qry, u_blk)

    def cond(carry):
        i, cm = carry
        return jnp.logical_and(i < qi, cm > SB_DEAD)

    def body(carry):
        i, _ = carry
        ki = qi - 1 - i
        st = pl.multiple_of(ki * tq, tq)
        block(k_ref[0, pl.ds(st, tq), :], vt_ref[0, ki], None, u_blk)
        return i + 1, c_max()

    _, cm = lax.while_loop(cond, body, (jnp.int32(0), c_max()))

    @pl.when(cm > SB_DEAD)
    def _():
        mkey = lax.broadcasted_iota(I32, (META_ROWS, tq), 0)
        block(kmeta_ref[0], vtmeta_ref[0, 0], mkey < N_META, later_matrix(META_ROWS))

    feat = lax.broadcasted_iota(I32, (LANES, tq), 0)
    o_ref[0] = jnp.where(feat < SB_DIM, a0_sc[...], a1_sc[...]).T


def _sb_attention(q, k, vt, kmeta, vtmeta, tq):
    b, l, _ = q.shape
    pairs = SB_W // LANES
    return pl.pallas_call(
        functools.partial(_sb_kernel, tq=tq),
        grid=(b, pairs, l // tq),
        in_specs=[
            pl.BlockSpec((1, tq, LANES), lambda bi, h, i: (bi, i, h)),
            pl.BlockSpec((1, l, LANES), lambda bi, h, i: (bi, 0, h)),
            pl.BlockSpec((1, l // tq, LANES, tq), lambda bi, h, i: (bi, 0, h, 0)),
            pl.BlockSpec((1, META_ROWS, LANES), lambda bi, h, i: (0, 0, h)),
            pl.BlockSpec((1, 1, LANES, META_ROWS), lambda bi, h, i: (0, 0, h, 0)),
        ],
        out_specs=pl.BlockSpec((1, tq, LANES), lambda bi, h, i: (bi, i, h)),
        out_shape=jax.ShapeDtypeStruct((b, l, SB_W), F32),
        scratch_shapes=[pltpu.VMEM((1, tq), F32), pltpu.VMEM((1, tq), F32),
                        pltpu.VMEM((LANES, tq), F32), pltpu.VMEM((LANES, tq), F32)],
        compiler_params=_cparams(("parallel", "parallel", "arbitrary")),
        name="sb",
    )(q, k, vt, kmeta, vtmeta)


def _post_kernel(x_ref, om_ref, os_ref, g1_ref, g2_ref, wo_ref, lg_ref, lb_ref, wr_ref, br_ref,
                 h1_ref, h1p_ref, lt_ref):
    n1 = _rms(om_ref[...], g1_ref[...]).astype(BF16)
    n2 = _rms(os_ref[...], g2_ref[...]).astype(BF16)
    a = _dot(n1, wo_ref[0:MLA_W, :]) + _dot(n2, wo_ref[MLA_W:MLA_W + SB_W, :])
    h1 = _layer_norm(DN_ALPHA * x_ref[...] + a, lg_ref[...], lb_ref[...])
    h1_ref[...] = h1
    h1p_ref[...] = _pack_rows(h1)
    lt_ref[...] = lax.dot_general(wr_ref[...], h1, (((1,), (1,)), ((), ())),
                                  precision=lax.Precision.HIGHEST, preferred_element_type=F32) + br_ref[...]


def _post(x2, om, osb, g1, g2, wo, lg, lb, wr_t, br, tm):
    t, d = x2.shape
    full = lambda a: pl.BlockSpec(a.shape, lambda i: (0,) * a.ndim)
    tok = lambda w: pl.BlockSpec((tm, w), lambda i: (i, 0))
    return pl.pallas_call(
        _post_kernel,
        grid=(t // tm,),
        in_specs=[tok(d), tok(MLA_W), tok(SB_W), full(g1), full(g2), full(wo), full(lg), full(lb), full(wr_t), full(br)],
        out_specs=[tok(d), tok(d // 2), pl.BlockSpec((N_EXPERTS, tm), lambda i: (0, i))],
        out_shape=[jax.ShapeDtypeStruct((t, d), F32), jax.ShapeDtypeStruct((t, d // 2), U32),
                   jax.ShapeDtypeStruct((N_EXPERTS, t), F32)],
        compiler_params=_cparams(("parallel",)),
        name="post",
    )(x2, om, osb, g1, g2, wo, lg, lb, wr_t, br)


def _route_kernel(lt_ref, ti_ref, gate_ref, rank_ref, cnt_ref, carry_sc, *, tt):
    @pl.when(pl.program_id(0) == 0)
    def _():
        carry_sc[...] = jnp.zeros_like(carry_sc)

    logits = lt_ref[...]
    eidx = lax.broadcasted_iota(I32, logits.shape, 0)
    sels, vals = [], []
    for k in range(TOP_K):
        mx = jnp.max(logits, axis=0, keepdims=True)
        idx = jnp.min(jnp.where(logits == mx, eidx, N_EXPERTS), axis=0, keepdims=True)
        sel = eidx == idx
        logits = jnp.where(sel, -jnp.inf, logits)
        ti_ref[k:k + 1, :] = idx
        sels.append(sel)
        vals.append(mx)
    exps = [jnp.exp(v - vals[0]) for v in vals]
    denom = exps[0] + exps[1] + exps[2] + exps[3]
    for k in range(TOP_K):
        gate_ref[k:k + 1, :] = exps[k] / denom

    chosen = sum(jnp.where(s, 1.0, 0.0) for s in sels)
    r = lax.broadcasted_iota(I32, (tt, tt), 0)
    c = lax.broadcasted_iota(I32, (tt, tt), 1)
    before = jnp.where(r < c, 1.0, 0.0).astype(BF16)
    rank = _dot(chosen.astype(BF16), before) + carry_sc[...]
    for k in range(TOP_K):
        rank_ref[k:k + 1, :] = jnp.sum(jnp.where(sels[k], rank, 0.0), axis=0, keepdims=True).astype(I32)
    carry_sc[...] += jnp.sum(chosen, axis=1, keepdims=True)
    cnt_ref[...] = jnp.broadcast_to(carry_sc[...], cnt_ref.shape)


def _route(lt, tt):
    e, t = lt.shape
    tok = pl.BlockSpec((TOP_K, tt), lambda i: (0, i))
    return pl.pallas_call(
        functools.partial(_route_kernel, tt=tt),
        grid=(t // tt,),
        in_specs=[pl.BlockSpec((e, tt), lambda i: (0, i))],
        out_specs=[tok, tok, tok, pl.BlockSpec((e, LANES), lambda i: (0, 0))],
        out_shape=[jax.ShapeDtypeStruct((TOP_K, t), I32), jax.ShapeDtypeStruct((TOP_K, t), F32),
                   jax.ShapeDtypeStruct((TOP_K, t), I32), jax.ShapeDtypeStruct((e, LANES), F32)],
        scratch_shapes=[pltpu.VMEM((e, 1), F32)],
        compiler_params=_cparams(("arbitrary",)),
        name="route",
    )(lt)


def _expert_kernel(be_ref, nu_ref, x_ref, wg_ref, bg_ref, wu_ref, bu_ref, wd_ref, bd_ref, y_ref,
                   wg_sc, wu_sc, wd_sc):
    i = pl.program_id(0)
    half = D_MODEL // 2
    active = i < nu_ref[0]

    @pl.when(jnp.logical_and(active, jnp.logical_or(i == 0, be_ref[i] != be_ref[jnp.maximum(i - 1, 0)])))
    def _():
        wg_sc[...] = wg_ref[0].astype(BF16)
        wu_sc[...] = wu_ref[0].astype(BF16)
        wd_sc[...] = wd_ref[0].astype(BF16)

    @pl.when(active)
    def _():
        x_lo, x_hi = _unpack_rows(x_ref[...])
        g = _dot(x_lo, wg_sc[0:half, :]) + _dot(x_hi, wg_sc[half:D_MODEL, :]) + bg_ref[0]
        u = _dot(x_lo, wu_sc[0:half, :]) + _dot(x_hi, wu_sc[half:D_MODEL, :]) + bu_ref[0]
        g = jnp.minimum(g, SWIGLU_LIMIT)
        u = jnp.clip(u, -SWIGLU_LIMIT, SWIGLU_LIMIT)
        a = (u + 1.0) * (g * (1.0 / (1.0 + jnp.exp(-SWIGLU_ALPHA * g))))
        y_ref[...] = _pack_rows(_dot(a.astype(BF16), wd_sc[...]) + bd_ref[0])

    @pl.when(i >= nu_ref[0])
    def _():
        y_ref[...] = jnp.zeros_like(y_ref)


def _experts(blk_e, n_used, buf, wg, bg, wu, bu, wd, bd):
    n_rows, d = buf.shape
    n_blk = n_rows // ROW_BLOCK
    wspec = lambda a: pl.BlockSpec((1,) + a.shape[1:], lambda i, be, nu: (be[i], 0, 0))
    rows = pl.BlockSpec((ROW_BLOCK, d), lambda i, be, nu: (i, 0))
    return pl.pallas_call(
        _expert_kernel,
        grid_spec=pltpu.PrefetchScalarGridSpec(
            num_scalar_prefetch=2,
            grid=(n_blk,),
            in_specs=[rows, wspec(wg), wspec(bg), wspec(wu), wspec(bu), wspec(wd), wspec(bd)],
            out_specs=rows,
            scratch_shapes=[pltpu.VMEM(wg.shape[1:], BF16), pltpu.VMEM(wu.shape[1:], BF16),
                            pltpu.VMEM(wd.shape[1:], BF16)],
        ),
        out_shape=jax.ShapeDtypeStruct((n_rows, d), U32),
        compiler_params=_cparams(("arbitrary",)),
        name="experts",
    )(blk_e, n_used, buf, wg, bg, wu, bu, wd, bd)


def _combine_kernel(h1_ref, y_ref, gate_ref, lg_ref, lb_ref, o_ref):
    gates = gate_ref[...]
    f_lo = f_hi = None
    for k in range(TOP_K):
        lo, hi = _unpack_rows(y_ref[k])
        g = gates[:, k:k + 1]
        f_lo = g * lo.astype(F32) if f_lo is None else f_lo + g * lo.astype(F32)
        f_hi = g * hi.astype(F32) if f_hi is None else f_hi + g * hi.astype(F32)
    f = jnp.concatenate([f_lo, f_hi], axis=1)
    o_ref[...] = _layer_norm(DN_ALPHA * h1_ref[...] + f, lg_ref[...], lb_ref[...])


def _combine(h1, yg, gates, lg, lb, tm):
    t, d = h1.shape
    full = lambda a: pl.BlockSpec(a.shape, lambda i: (0,) * a.ndim)
    return pl.pallas_call(
        _combine_kernel,
        grid=(t // tm,),
        in_specs=[pl.BlockSpec((tm, d), lambda i: (i, 0)), pl.BlockSpec((TOP_K, tm, d // 2), lambda i: (0, i, 0)),
                  pl.BlockSpec((tm, TOP_K), lambda i: (i, 0)), full(lg), full(lb)],
        out_specs=pl.BlockSpec((tm, d), lambda i: (i, 0)),
        out_shape=jax.ShapeDtypeStruct((t, d), F32),
        compiler_params=_cparams(("parallel",)),
        name="combine",
    )(h1, yg, gates, lg, lb)


def _rope_tables(positions):
    half = QK_ROPE // 2
    freqs = ROPE_THETA ** (-jnp.arange(half, dtype=F32) * 2.0 / QK_ROPE)
    ang = positions.astype(F32)[:, None] * freqs[None, :]
    pad = jnp.zeros((positions.shape[0], LANES - QK_ROPE), F32)
    cos, sin = jnp.cos(ang), jnp.sin(ang)
    return jnp.concatenate([cos, cos, pad], axis=1), jnp.concatenate([sin, sin, pad], axis=1)


def _rope_cols(w):
    half = QK_ROPE // 2
    x1, x2 = w[:, :half], w[:, half:]
    pad = jnp.zeros((w.shape[0], LANES - QK_ROPE), w.dtype)
    return jnp.concatenate([x1, x2, pad], axis=1), jnp.concatenate([-x2, x1, pad], axis=1)


def _layout_weights(w_in, q_norm_g, w_uq, kv_norm_g, w_ukv):
    c_q = w_in[:, 0:Q_LORA]
    c_kv = w_in[:, Q_LORA:Q_LORA + KV_LORA]
    o = Q_LORA + KV_LORA
    k_r = w_in[:, o:o + QK_ROPE]
    o += QK_ROPE
    sb_q = w_in[:, o:o + SB_W] * (SB_DIM ** -0.5)
    sb_k = w_in[:, o + SB_W:o + 2 * SB_W]
    sb_v = w_in[:, o + 2 * SB_W:o + 3 * SB_W]
    kr, kr_rot = _rope_cols(k_r)
    w1 = jnp.concatenate([c_q, c_kv, sb_q, sb_k, kr, kr_rot], axis=1).astype(BF16)
    wsv_t = sb_v.T.astype(BF16)

    uq = w_uq.reshape(Q_LORA, MLA_HEADS, QK_NOPE + QK_ROPE)
    nope = uq[:, :, :QK_NOPE].reshape(Q_LORA, MLA_HEADS * QK_NOPE)
    ropes = [_rope_cols(uq[:, h, QK_NOPE:]) for h in range(MLA_HEADS)]
    wq = jnp.concatenate([nope] + [r[0] for r in ropes] + [r[1] for r in ropes], axis=1).astype(BF16)

    ukv = w_ukv.reshape(KV_LORA, MLA_HEADS, QK_NOPE + V_DIM)
    wk = ukv[:, :, :QK_NOPE].reshape(KV_LORA, -1).astype(BF16)
    wv_t = ukv[:, :, QK_NOPE:].reshape(KV_LORA, -1).T.astype(BF16)
    return (w1, wsv_t, q_norm_g.reshape(1, -1), kv_norm_g.reshape(1, -1), wq, wk, wv_t)


def kernel(x, meta_tokens, w_in, q_norm_g, w_uq, kv_norm_g, w_ukv, mla_out_g, sb_out_g, w_o, ln1_g, ln1_b,
           w_router, b_router, w_gate, b_gate, w_up, b_up, w_down, b_down, ln2_g, ln2_b):
    b, seq, d = x.shape
    t = b * seq
    tq_mla = min(512, seq)
    tk_mla = min(512, seq)
    tq_sb = min(256, seq)
    tm_proj = min(512, seq)
    tm = min(256, seq)
    tt = min(512, t)
    row = lambda v: v.reshape(1, -1)

    w = _layout_weights(w_in[0], q_norm_g[0], w_uq[0], kv_norm_g[0], w_ukv[0])
    cos_x, sin_x = _rope_tables(N_META + jnp.arange(seq))
    cos_m, sin_m = _rope_tables(jnp.arange(META_ROWS))
    meta = jnp.zeros((1, META_ROWS, d), x.dtype).at[0, :N_META].set(meta_tokens.astype(x.dtype))
    q_mla, k_mla, vt_mla, sb_q, sb_k, sb_vt = _project(x, w, cos_x, sin_x, tm_proj, tk_mla, tq_sb)
    _, km_meta, vtm_meta, _, sk_meta, svt_meta = _project(meta, w, cos_m, sin_m, META_ROWS, META_ROWS, META_ROWS)
    o_mla = _mla_attention(q_mla, k_mla, vt_mla, km_meta, vtm_meta, tq_mla, tk_mla)
    o_sb = _sb_attention(sb_q, sb_k, sb_vt, sk_meta, svt_meta, tq_sb)

    h1, h1p, logits_t = _post(x.reshape(t, d), o_mla.reshape(t, MLA_W), o_sb.reshape(t, SB_W),
                              row(mla_out_g[0]), row(sb_out_g[0]), w_o[0].astype(BF16),
                              row(ln1_g[0]), row(ln1_b[0]), w_router[0].T, b_router[0].reshape(-1, 1), tm)

    top_i, gates, rank, cnt = _route(logits_t, tt)
    counts = cnt[:, 0].astype(I32)
    padded = (counts + ROW_BLOCK - 1) // ROW_BLOCK * ROW_BLOCK
    p_end = jnp.cumsum(padded)
    p_start = p_end - padded
    expert_ids = jnp.arange(N_EXPERTS, dtype=I32)[:, None, None]
    dest = jnp.sum(jnp.where(top_i[None] == expert_ids, p_start[:, None, None], 0), axis=0) + rank
    n_blk = (t * TOP_K + N_EXPERTS * (ROW_BLOCK - 1) + ROW_BLOCK - 1) // ROW_BLOCK
    blk_e = jnp.minimum(jnp.sum(jnp.arange(n_blk)[:, None] * ROW_BLOCK >= p_end[None, :], axis=1),
                        N_EXPERTS - 1).astype(I32)
    n_used = (p_end[-1:] // ROW_BLOCK).astype(I32)

    tok_ids = jnp.broadcast_to(jnp.arange(t, dtype=I32)[None, :], (TOP_K, t))
    src = jnp.zeros((n_blk * ROW_BLOCK,), I32).at[dest.reshape(-1)].set(tok_ids.reshape(-1))
    bufp = h1p[src]
    yp = _experts(blk_e, n_used, bufp,
                  w_gate[0], b_gate[0].reshape(N_EXPERTS, 1, D_FF),
                  w_up[0], b_up[0].reshape(N_EXPERTS, 1, D_FF),
                  w_down[0], b_down[0].reshape(N_EXPERTS, 1, d))
    ygp = yp[dest]

    out = _combine(h1, ygp, gates.T, row(ln2_g[0]), row(ln2_b[0]), tm)
    return out.reshape(b, seq, d)
```

```python
import functools

import jax
import jax.numpy as jnp
from jax import lax
from jax.experimental import pallas as pl
from jax.experimental.pallas import tpu as pltpu
from jax.experimental.pallas import tpu_sc as plsc

F32 = jnp.float32
BF16 = jnp.bfloat16
I32 = jnp.int32
U32 = jnp.uint32

D_MODEL = 1024
N_META = 16
MLA_HEADS = 4
QK_NOPE = 128
QK_ROPE = 64
V_DIM = 128
Q_LORA = 256
KV_LORA = 256
ROPE_THETA = 10000.0
SB_HEADS = 8
SB_DIM = 64
MLA_W = MLA_HEADS * V_DIM
SB_W = SB_HEADS * SB_DIM
N_EXPERTS = 32
TOP_K = 4
D_FF = 1024
SWIGLU_LIMIT = 7.0
SWIGLU_ALPHA = 1.702
DEPTH = 1
DN_ALPHA = (2 * DEPTH) ** 0.25
LN_EPS = 1e-5
RMS_EPS = 1e-6
NEG_INF = -1e30

LANES = 128
META_ROWS = 128
QK_PAD = 2 * LANES
ROW_BLOCK = 512
LOG2_E = 1.4426950408889634
ROW_WORDS = D_MODEL // 4
SC_WINDOW = 128
VMEM_LIMIT = 56 * 1024 * 1024
SB_DEAD = -105.0


def _cparams(sem):
    return pltpu.CompilerParams(dimension_semantics=sem, vmem_limit_bytes=VMEM_LIMIT)


def _rms(x, g):
    return x * lax.rsqrt(jnp.mean(x * x, axis=-1, keepdims=True) + RMS_EPS) * g


def _layer_norm(x, g, b):
    mu = jnp.mean(x, axis=-1, keepdims=True)
    xc = x - mu
    var = jnp.mean(xc * xc, axis=-1, keepdims=True)
    return xc * lax.rsqrt(var + LN_EPS) * g + b


def _dot(a, b):
    return jnp.dot(a, b, preferred_element_type=F32)


def _dot_nt(a, b):
    return lax.dot_general(a, b, (((1,), (1,)), ((), ())), preferred_element_type=F32)


def _bf16_bits(x):
    bits = pltpu.bitcast(x, U32)
    return (bits + jnp.uint32(0x7FFF) + ((bits >> 16) & jnp.uint32(1))) & jnp.uint32(0xFFFF0000)


def _pack_rows(x):
    n = x.shape[1] // 2
    return _bf16_bits(x[:, n:]) | (_bf16_bits(x[:, :n]) >> 16)


def _unpack_rows(p):
    lo = pltpu.bitcast(p << 16, F32)
    hi = pltpu.bitcast(p & jnp.uint32(0xFFFF0000), F32)
    return lo.astype(BF16), hi.astype(BF16)


def _store_halves(ref, packed):
    ref[0] = packed[:, :ROW_WORDS]
    ref[1] = packed[:, ROW_WORDS:]


def _unpack_halves(ref_halves):
    lo0, hi0 = _unpack_rows(ref_halves[0])
    lo1, hi1 = _unpack_rows(ref_halves[1])
    return lo0, lo1, hi0, hi1


def _proj_kernel(x_ref, w1_ref, wsv_ref, qg_ref, kvg_ref, wq_ref, wk_ref, wv_ref, cos_ref, sin_ref,
                 qm_ref, km_ref, vmt_ref, sq_ref, sk_ref, svt_ref, *, tm, tkm, tks):
    xb = x_ref[0].astype(BF16)
    p = _dot(xb, w1_ref[...])
    sq_ref[0] = p[:, 512:1024].astype(BF16)
    sk_ref[0] = p[:, 1024:1536].astype(BF16)
    svt = _dot_nt(wsv_ref[...], xb).astype(BF16)
    for j in range(tm // tks):
        svt_ref[0, j] = svt[:, j * tks:(j + 1) * tks]
    cos = cos_ref[...]
    sin = sin_ref[...]
    k_rope = (p[:, 1536:1664] * cos + p[:, 1664:1792] * sin).astype(BF16)
    nq = _rms(p[:, 0:256], qg_ref[...]).astype(BF16)
    nkv = _rms(p[:, 256:512], kvg_ref[...]).astype(BF16)
    qq = _dot(nq, wq_ref[...])
    kn = _dot(nkv, wk_ref[...])
    vmt = _dot_nt(wv_ref[...], nkv).astype(BF16)
    for j in range(tm // tkm):
        vmt_ref[0, j] = vmt[:, j * tkm:(j + 1) * tkm]
    scale = (QK_NOPE + QK_ROPE) ** -0.5 * LOG2_E
    for h in range(MLA_HEADS):
        lo = h * QK_PAD
        qm_ref[0, :, lo:lo + LANES] = (qq[:, h * LANES:(h + 1) * LANES] * scale).astype(BF16)
        q_rope = qq[:, 512 + h * LANES:512 + (h + 1) * LANES] * cos + qq[:, 1024 + h * LANES:1024 + (h + 1) * LANES] * sin
        qm_ref[0, :, lo + LANES:lo + QK_PAD] = (q_rope * scale).astype(BF16)
        km_ref[0, :, lo:lo + LANES] = kn[:, h * LANES:(h + 1) * LANES].astype(BF16)
        km_ref[0, :, lo + LANES:lo + QK_PAD] = k_rope


def _project(x, w, cos, sin, tm, tkm, tks):
    b, l, d = x.shape
    full = lambda a: pl.BlockSpec(a.shape, lambda bi, i: (0,) * a.ndim)
    tok = lambda width: pl.BlockSpec((1, tm, width), lambda bi, i: (bi, i, 0))
    tab = pl.BlockSpec((tm, LANES), lambda bi, i: (i, 0))
    vt = lambda width, tk: pl.BlockSpec((1, tm // tk, width, tk), lambda bi, i: (bi, i, 0, 0))
    qk_w = MLA_HEADS * QK_PAD
    return pl.pallas_call(
        functools.partial(_proj_kernel, tm=tm, tkm=tkm, tks=tks),
        grid=(b, l // tm),
        in_specs=[tok(d)] + [full(a) for a in w] + [tab, tab],
        out_specs=[tok(qk_w), tok(qk_w), vt(MLA_W, tkm), tok(SB_W), tok(SB_W), vt(SB_W, tks)],
        out_shape=[jax.ShapeDtypeStruct((b, l, qk_w), BF16), jax.ShapeDtypeStruct((b, l, qk_w), BF16),
                   jax.ShapeDtypeStruct((b, l // tkm, MLA_W, tkm), BF16),
                   jax.ShapeDtypeStruct((b, l, SB_W), BF16), jax.ShapeDtypeStruct((b, l, SB_W), BF16),
                   jax.ShapeDtypeStruct((b, l // tks, SB_W, tks), BF16)],
        compiler_params=_cparams(("parallel", "parallel")),
        name="proj",
    )(x, *w, cos, sin)


def _mla_kernel(q_ref, k_ref, vt_ref, kmeta_ref, vtmeta_ref, o_ref, m_sc, l_sc, acc_sc, *, tq, tk):
    qi = pl.program_id(2)
    q = q_ref[0]

    s = _dot_nt(kmeta_ref[0], q)
    key = lax.broadcasted_iota(I32, s.shape, 0)
    s = jnp.where(key < N_META, s, NEG_INF)
    m0 = jnp.max(s, axis=0, keepdims=True)
    p0 = jnp.exp2(s - m0)
    m_sc[...] = m0
    l_sc[...] = jnp.sum(p0, axis=0, keepdims=True)
    acc_sc[...] = _dot(vtmeta_ref[0, 0], p0.astype(BF16))

    def update(s, vt):
        m_prev = m_sc[...]
        m_new = jnp.maximum(m_prev, jnp.max(s, axis=0, keepdims=True))
        alpha = jnp.exp2(m_prev - m_new)
        p = jnp.exp2(s - m_new)
        l_sc[...] = alpha * l_sc[...] + jnp.sum(p, axis=0, keepdims=True)
        acc_sc[...] = alpha * acc_sc[...] + _dot(vt, p.astype(BF16))
        m_sc[...] = m_new

    def scores(ki):
        start = pl.multiple_of(ki * tk, tk)
        return _dot_nt(k_ref[0, pl.ds(start, tk), :], q)

    def body(ki, s):
        s_next = scores(ki + 1)
        update(s, vt_ref[0, ki])
        return s_next

    n_diag = tq // tk
    first_diag = qi * n_diag
    s = lax.fori_loop(0, first_diag, body, scores(0))
    key = lax.broadcasted_iota(I32, s.shape, 0)
    qry = lax.broadcasted_iota(I32, s.shape, 1)
    for j in range(n_diag):
        s_next = scores(first_diag + j + 1) if j + 1 < n_diag else None
        update(jnp.where(key + j * tk <= qry, s, NEG_INF), vt_ref[0, first_diag + j])
        s = s_next
    o_ref[0] = (acc_sc[...] / l_sc[...]).T


def _mla_attention(q, k, vt, kmeta, vtmeta, tq, tk):
    b, l, _ = q.shape
    return pl.pallas_call(
        functools.partial(_mla_kernel, tq=tq, tk=tk),
        grid=(b, MLA_HEADS, l // tq),
        in_specs=[
            pl.BlockSpec((1, tq, QK_PAD), lambda bi, h, i: (bi, i, h)),
            pl.BlockSpec((1, l, QK_PAD), lambda bi, h, i: (bi, 0, h)),
            pl.BlockSpec((1, l // tk, V_DIM, tk), lambda bi, h, i: (bi, 0, h, 0)),
            pl.BlockSpec((1, META_ROWS, QK_PAD), lambda bi, h, i: (0, 0, h)),
            pl.BlockSpec((1, 1, V_DIM, META_ROWS), lambda bi, h, i: (0, 0, h, 0)),
        ],
        out_specs=pl.BlockSpec((1, tq, V_DIM), lambda bi, h, i: (bi, i, h)),
        out_shape=jax.ShapeDtypeStruct((b, l, MLA_W), F32),
        scratch_shapes=[pltpu.VMEM((1, tq), F32), pltpu.VMEM((1, tq), F32), pltpu.VMEM((V_DIM, tq), F32)],
        compiler_params=_cparams(("parallel", "parallel", "arbitrary")),
        name="mla",
    )(q, k, vt, kmeta, vtmeta)


def _sb_kernel(q_ref, k_ref, vt_ref, kmeta_ref, vtmeta_ref, o_ref, c0_sc, c1_sc, a0_sc, a1_sc, *, tq):
    qi = pl.program_id(2)
    q = q_ref[0]
    lane = lax.broadcasted_iota(I32, q.shape, 1)
    zero = jnp.zeros_like(q)
    q_heads = (jnp.where(lane < SB_DIM, q, zero), jnp.where(lane >= SB_DIM, q, zero))
    c_refs = (c0_sc, c1_sc)
    acc_refs = (a0_sc, a1_sc)

    def later_matrix(n):
        r = lax.broadcasted_iota(I32, (n, n), 0)
        c = lax.broadcasted_iota(I32, (n, n), 1)
        return jnp.where(c > r, 1.0, 0.0).astype(BF16)

    for h in range(2):
        c_refs[h][...] = jnp.zeros_like(c_refs[h])
        acc_refs[h][...] = jnp.zeros_like(acc_refs[h])

    def block(kblk, vt, valid, u):
        heads = range(2)
        z = [_dot_nt(kblk, q_heads[h]) for h in heads]
        log_beta, log_keep = [], []
        for h in heads:
            softplus = jnp.maximum(z[h], 0.0) + jnp.log(1.0 + jnp.exp(-jnp.abs(z[h])))
            lk = -softplus
            if valid is not None:
                lk = jnp.where(valid, lk, 0.0)
            log_keep.append(lk)
            log_beta.append(z[h] - softplus)
        later = []
        for h in heads:
            hi = log_keep[h].astype(BF16)
            lo = (log_keep[h] - hi.astype(F32)).astype(BF16)
            later.append(_dot(u, hi) + _dot(u, lo) + c_refs[h][...])
        for h in heads:
            w = jnp.exp(log_beta[h] + later[h])
            if valid is not None:
                w = jnp.where(valid, w, 0.0)
            acc_refs[h][...] += _dot(vt, w.astype(BF16))
            c_refs[h][...] = later[h][0:1, :] + log_keep[h][0:1, :]

    def c_max():
        return jnp.maximum(jnp.max(c0_sc[...]), jnp.max(c1_sc[...]))

    u_blk = later_matrix(tq)
    start = pl.multiple_of(qi * tq, tq)
    key = lax.broadcasted_iota(I32, (tq, tq), 0)
    qry = lax.broadcasted_iota(I32, (tq, tq), 1)
    block(k_ref[0, pl.ds(start, tq), :], vt_ref[0, qi], key < qry, u_blk)

    def cond(carry):
        i, cm = carry
        return jnp.logical_and(i < qi, cm > SB_DEAD)

    def body(carry):
        i, _ = carry
        ki = qi - 1 - i
        st = pl.multiple_of(ki * tq, tq)
        block(k_ref[0, pl.ds(st, tq), :], vt_ref[0, ki], None, u_blk)
        return i + 1, c_max()

    _, cm = lax.while_loop(cond, body, (jnp.int32(0), c_max()))

    @pl.when(cm > SB_DEAD)
    def _():
        mkey = lax.broadcasted_iota(I32, (META_ROWS, tq), 0)
        block(kmeta_ref[0], vtmeta_ref[0, 0], mkey < N_META, later_matrix(META_ROWS))

    feat = lax.broadcasted_iota(I32, (LANES, tq), 0)
    o_ref[0] = jnp.where(feat < SB_DIM, a0_sc[...], a1_sc[...]).T


def _sb_attention(q, k, vt, kmeta, vtmeta, tq):
    b, l, _ = q.shape
    pairs = SB_W // LANES
    return pl.pallas_call(
        functools.partial(_sb_kernel, tq=tq),
        grid=(b, pairs, l // tq),
        in_specs=[
            pl.BlockSpec((1, tq, LANES), lambda bi, h, i: (bi, i, h)),
            pl.BlockSpec((1, l, LANES), lambda bi, h, i: (bi, 0, h)),
            pl.BlockSpec((1, l // tq, LANES, tq), lambda bi, h, i: (bi, 0, h, 0)),
            pl.BlockSpec((1, META_ROWS, LANES), lambda bi, h, i: (0, 0, h)),
            pl.BlockSpec((1, 1, LANES, META_ROWS), lambda bi, h, i: (0, 0, h, 0)),
        ],
        out_specs=pl.BlockSpec((1, tq, LANES), lambda bi, h, i: (bi, i, h)),
        out_shape=jax.ShapeDtypeStruct((b, l, SB_W), F32),
        scratch_shapes=[pltpu.VMEM((1, tq), F32), pltpu.VMEM((1, tq), F32),
                        pltpu.VMEM((LANES, tq), F32), pltpu.VMEM((LANES, tq), F32)],
        compiler_params=_cparams(("parallel", "parallel", "arbitrary")),
        name="sb",
    )(q, k, vt, kmeta, vtmeta)


def _post_kernel(x_ref, om_ref, os_ref, g1_ref, g2_ref, wo_ref, lg_ref, lb_ref, wr_ref, br_ref,
                 h1_ref, h1p_ref, lt_ref):
    n1 = _rms(om_ref[...], g1_ref[...]).astype(BF16)
    n2 = _rms(os_ref[...], g2_ref[...]).astype(BF16)
    a = _dot(n1, wo_ref[0:MLA_W, :]) + _dot(n2, wo_ref[MLA_W:MLA_W + SB_W, :])
    h1 = _layer_norm(DN_ALPHA * x_ref[...] + a, lg_ref[...], lb_ref[...])
    h1_ref[...] = h1
    _store_halves(h1p_ref, _pack_rows(h1))
    lt_ref[...] = lax.dot_general(wr_ref[...], h1, (((1,), (1,)), ((), ())),
                                  precision=lax.Precision.HIGHEST, preferred_element_type=F32) + br_ref[...]


def _post(x2, om, osb, g1, g2, wo, lg, lb, wr_t, br, tm):
    t, d = x2.shape
    full = lambda a: pl.BlockSpec(a.shape, lambda i: (0,) * a.ndim)
    tok = lambda w: pl.BlockSpec((tm, w), lambda i: (i, 0))
    return pl.pallas_call(
        _post_kernel,
        grid=(t // tm,),
        in_specs=[tok(d), tok(MLA_W), tok(SB_W), full(g1), full(g2), full(wo), full(lg), full(lb), full(wr_t), full(br)],
        out_specs=[tok(d), pl.BlockSpec((2, tm, ROW_WORDS), lambda i: (0, i, 0)),
                   pl.BlockSpec((N_EXPERTS, tm), lambda i: (0, i))],
        out_shape=[jax.ShapeDtypeStruct((t, d), F32), jax.ShapeDtypeStruct((2, t, ROW_WORDS), U32),
                   jax.ShapeDtypeStruct((N_EXPERTS, t), F32)],
        compiler_params=_cparams(("parallel",)),
        name="post",
    )(x2, om, osb, g1, g2, wo, lg, lb, wr_t, br)


def _route_kernel(lt_ref, ti_ref, gate_ref, rank_ref, cnt_ref, carry_sc, *, tt):
    @pl.when(pl.program_id(0) == 0)
    def _():
        carry_sc[...] = jnp.zeros_like(carry_sc)

    logits = lt_ref[...]
    eidx = lax.broadcasted_iota(I32, logits.shape, 0)
    sels, vals = [], []
    for k in range(TOP_K):
        mx = jnp.max(logits, axis=0, keepdims=True)
        idx = jnp.min(jnp.where(logits == mx, eidx, N_EXPERTS), axis=0, keepdims=True)
        sel = eidx == idx
        logits = jnp.where(sel, -jnp.inf, logits)
        ti_ref[k:k + 1, :] = idx
        sels.append(sel)
        vals.append(mx)
    exps = [jnp.exp(v - vals[0]) for v in vals]
    denom = exps[0] + exps[1] + exps[2] + exps[3]
    for k in range(TOP_K):
        gate_ref[k:k + 1, :] = exps[k] / denom

    chosen = sum(jnp.where(s, 1.0, 0.0) for s in sels)
    r = lax.broadcasted_iota(I32, (tt, tt), 0)
    c = lax.broadcasted_iota(I32, (tt, tt), 1)
    before = jnp.where(r < c, 1.0, 0.0).astype(BF16)
    rank = _dot(chosen.astype(BF16), before) + carry_sc[...]
    for k in range(TOP_K):
        rank_ref[k:k + 1, :] = jnp.sum(jnp.where(sels[k], rank, 0.0), axis=0, keepdims=True).astype(I32)
    carry_sc[...] += jnp.sum(chosen, axis=1, keepdims=True)
    cnt_ref[...] = jnp.broadcast_to(carry_sc[...], cnt_ref.shape)


def _route(lt, tt):
    e, t = lt.shape
    tok = pl.BlockSpec((TOP_K, tt), lambda i: (0, i))
    return pl.pallas_call(
        functools.partial(_route_kernel, tt=tt),
        grid=(t // tt,),
        in_specs=[pl.BlockSpec((e, tt), lambda i: (0, i))],
        out_specs=[tok, tok, tok, pl.BlockSpec((e, LANES), lambda i: (0, 0))],
        out_shape=[jax.ShapeDtypeStruct((TOP_K, t), I32), jax.ShapeDtypeStruct((TOP_K, t), F32),
                   jax.ShapeDtypeStruct((TOP_K, t), I32), jax.ShapeDtypeStruct((e, LANES), F32)],
        scratch_shapes=[pltpu.VMEM((e, 1), F32)],
        compiler_params=_cparams(("arbitrary",)),
        name="route",
    )(lt)


def _sc_mesh():
    return plsc.VectorSubcoreMesh(core_axis_name="c", subcore_axis_name="s")


def _sc_gather_rows(x, idx):
    m = idx.shape[0]
    w = x.shape[1]

    @pl.kernel(out_type=jax.ShapeDtypeStruct((m, w), x.dtype), mesh=_sc_mesh())
    def gather(x_hbm, i_hbm, o_hbm):
        def body(i_vmem, o_vmem):
            pltpu.sync_copy(x_hbm.at[i_vmem.at[0]], o_vmem)

        pltpu.emit_pipeline(
            body, grid=(m // SC_WINDOW,),
            in_specs=[pl.BlockSpec((1, SC_WINDOW), lambda i: (0, i))],
            out_specs=[pl.BlockSpec((SC_WINDOW, w), lambda i: (i, 0))],
            core_axis_name=("c", "s"), dimension_semantics=(pltpu.PARALLEL,),
        )(i_hbm, o_hbm)

    return gather(x, idx.reshape(1, m))


def _sc_scatter_rows(x, dest, n_rows):
    kk, t = dest.shape
    w = x.shape[1]
    nt = t // SC_WINDOW

    @pl.kernel(out_type=jax.ShapeDtypeStruct((n_rows, w), x.dtype), mesh=_sc_mesh(), scratch_types=[])
    def scatter(x_hbm, i_hbm, o_hbm):
        def body(x_vmem, i_vmem):
            pltpu.sync_copy(x_vmem, o_hbm.at[i_vmem.at[0]])

        pltpu.emit_pipeline(
            body, grid=(kk * nt,),
            in_specs=[pl.BlockSpec((SC_WINDOW, w), lambda i: (i % nt, 0)),
                      pl.BlockSpec((1, SC_WINDOW), lambda i: (i // nt, i % nt))],
            out_specs=[],
            core_axis_name=("c", "s"), dimension_semantics=(pltpu.PARALLEL,),
        )(x_hbm, i_hbm)

    return scatter(x, dest)


def _expert_kernel(be_ref, nu_ref, nv_ref, x_ref, wg_ref, bg_ref, wu_ref, bu_ref, wd_ref, bd_ref, y_ref,
                   wg_sc, wu_sc, wd_sc):
    i = pl.program_id(0)
    active = i < nu_ref[0]

    @pl.when(jnp.logical_and(active, jnp.logical_or(i == 0, be_ref[i] != be_ref[jnp.maximum(i - 1, 0)])))
    def _():
        wg_sc[...] = wg_ref[0].astype(BF16)
        wu_sc[...] = wu_ref[0].astype(BF16)
        wd_sc[...] = wd_ref[0].astype(BF16)

    @pl.when(active)
    def _():
        live = lax.broadcasted_iota(I32, (ROW_BLOCK, ROW_WORDS), 0) < nv_ref[i]
        zero = jnp.zeros((ROW_BLOCK, ROW_WORDS), U32)
        xs = _unpack_halves((jnp.where(live, x_ref[0], zero), jnp.where(live, x_ref[1], zero)))
        cw = D_MODEL // len(xs)

        def proj(w_sc, b_ref):
            acc = b_ref[0]
            for j, xj in enumerate(xs):
                acc = acc + _dot(xj, w_sc[j * cw:(j + 1) * cw, :])
            return acc

        g = jnp.minimum(proj(wg_sc, bg_ref), SWIGLU_LIMIT)
        u = jnp.clip(proj(wu_sc, bu_ref), -SWIGLU_LIMIT, SWIGLU_LIMIT)
        a = (u + 1.0) * (g * (1.0 / (1.0 + jnp.exp(-SWIGLU_ALPHA * g))))
        _store_halves(y_ref, _pack_rows(_dot(a.astype(BF16), wd_sc[...]) + bd_ref[0]))

    @pl.when(i >= nu_ref[0])
    def _():
        y_ref[...] = jnp.zeros_like(y_ref)


def _experts(blk_e, n_used, n_valid, buf, wg, bg, wu, bu, wd, bd):
    _, n_rows, d = buf.shape
    n_blk = n_rows // ROW_BLOCK
    wspec = lambda a: pl.BlockSpec((1,) + a.shape[1:], lambda i, be, nu, nv: (be[i], 0, 0))
    rows = pl.BlockSpec((2, ROW_BLOCK, d), lambda i, be, nu, nv: (0, i, 0))
    return pl.pallas_call(
        _expert_kernel,
        grid_spec=pltpu.PrefetchScalarGridSpec(
            num_scalar_prefetch=3,
            grid=(n_blk,),
            in_specs=[rows, wspec(wg), wspec(bg), wspec(wu), wspec(bu), wspec(wd), wspec(bd)],
            out_specs=rows,
            scratch_shapes=[pltpu.VMEM(wg.shape[1:], BF16), pltpu.VMEM(wu.shape[1:], BF16),
                            pltpu.VMEM(wd.shape[1:], BF16)],
        ),
        out_shape=jax.ShapeDtypeStruct((2, n_rows, d), U32),
        compiler_params=_cparams(("arbitrary",)),
        name="experts",
    )(blk_e, n_used, n_valid, buf, wg, bg, wu, bu, wd, bd)


def _combine_kernel(h1_ref, y_ref, gate_ref, lg_ref, lb_ref, o_ref):
    gates = gate_ref[...]
    parts = None
    for k in range(TOP_K):
        g = gates[:, k:k + 1]
        ys = [g * c.astype(F32) for c in _unpack_halves((y_ref[0, k], y_ref[1, k]))]
        parts = ys if parts is None else [p + y for p, y in zip(parts, ys)]
    f = jnp.concatenate(parts, axis=1)
    o_ref[...] = _layer_norm(DN_ALPHA * h1_ref[...] + f, lg_ref[...], lb_ref[...])


def _combine(h1, yg, gates, lg, lb, tm):
    t, d = h1.shape
    full = lambda a: pl.BlockSpec(a.shape, lambda i: (0,) * a.ndim)
    return pl.pallas_call(
        _combine_kernel,
        grid=(t // tm,),
        in_specs=[pl.BlockSpec((tm, d), lambda i: (i, 0)),
                  pl.BlockSpec((2, TOP_K, tm, ROW_WORDS), lambda i: (0, 0, i, 0)),
                  pl.BlockSpec((tm, TOP_K), lambda i: (i, 0)), full(lg), full(lb)],
        out_specs=pl.BlockSpec((tm, d), lambda i: (i, 0)),
        out_shape=jax.ShapeDtypeStruct((t, d), F32),
        compiler_params=_cparams(("parallel",)),
        name="combine",
    )(h1, yg, gates, lg, lb)


def _rope_tables(positions):
    half = QK_ROPE // 2
    freqs = ROPE_THETA ** (-jnp.arange(half, dtype=F32) * 2.0 / QK_ROPE)
    ang = positions.astype(F32)[:, None] * freqs[None, :]
    pad = jnp.zeros((positions.shape[0], LANES - QK_ROPE), F32)
    cos, sin = jnp.cos(ang), jnp.sin(ang)
    return jnp.concatenate([cos, cos, pad], axis=1), jnp.concatenate([sin, sin, pad], axis=1)


def _rope_cols(w):
    half = QK_ROPE // 2
    x1, x2 = w[:, :half], w[:, half:]
    pad = jnp.zeros((w.shape[0], LANES - QK_ROPE), w.dtype)
    return jnp.concatenate([x1, x2, pad], axis=1), jnp.concatenate([-x2, x1, pad], axis=1)


def _layout_weights(w_in, q_norm_g, w_uq, kv_norm_g, w_ukv):
    c_q = w_in[:, 0:Q_LORA]
    c_kv = w_in[:, Q_LORA:Q_LORA + KV_LORA]
    o = Q_LORA + KV_LORA
    k_r = w_in[:, o:o + QK_ROPE]
    o += QK_ROPE
    sb_q = w_in[:, o:o + SB_W] * (SB_DIM ** -0.5)
    sb_k = w_in[:, o + SB_W:o + 2 * SB_W]
    sb_v = w_in[:, o + 2 * SB_W:o + 3 * SB_W]
    kr, kr_rot = _rope_cols(k_r)
    w1 = jnp.concatenate([c_q, c_kv, sb_q, sb_k, kr, kr_rot], axis=1).astype(BF16)
    wsv_t = sb_v.T.astype(BF16)

    uq = w_uq.reshape(Q_LORA, MLA_HEADS, QK_NOPE + QK_ROPE)
    nope = uq[:, :, :QK_NOPE].reshape(Q_LORA, MLA_HEADS * QK_NOPE)
    ropes = [_rope_cols(uq[:, h, QK_NOPE:]) for h in range(MLA_HEADS)]
    wq = jnp.concatenate([nope] + [r[0] for r in ropes] + [r[1] for r in ropes], axis=1).astype(BF16)

    ukv = w_ukv.reshape(KV_LORA, MLA_HEADS, QK_NOPE + V_DIM)
    wk = ukv[:, :, :QK_NOPE].reshape(KV_LORA, -1).astype(BF16)
    wv_t = ukv[:, :, QK_NOPE:].reshape(KV_LORA, -1).T.astype(BF16)
    return (w1, wsv_t, q_norm_g.reshape(1, -1), kv_norm_g.reshape(1, -1), wq, wk, wv_t)


def kernel(x, meta_tokens, w_in, q_norm_g, w_uq, kv_norm_g, w_ukv, mla_out_g, sb_out_g, w_o, ln1_g, ln1_b,
           w_router, b_router, w_gate, b_gate, w_up, b_up, w_down, b_down, ln2_g, ln2_b):
    b, seq, d = x.shape
    t = b * seq
    tq_mla = min(512, seq)
    tk_mla = min(512, seq)
    tq_sb = min(256, seq)
    tm_proj = min(512, seq)
    tm = min(256, seq)
    tt = min(512, t)
    row = lambda v: v.reshape(1, -1)

    w = _layout_weights(w_in[0], q_norm_g[0], w_uq[0], kv_norm_g[0], w_ukv[0])
    cos_x, sin_x = _rope_tables(N_META + jnp.arange(seq))
    cos_m, sin_m = _rope_tables(jnp.arange(META_ROWS))
    meta = jnp.zeros((1, META_ROWS, d), x.dtype).at[0, :N_META].set(meta_tokens.astype(x.dtype))
    q_mla, k_mla, vt_mla, sb_q, sb_k, sb_vt = _project(x, w, cos_x, sin_x, tm_proj, tk_mla, tq_sb)
    _, km_meta, vtm_meta, _, sk_meta, svt_meta = _project(meta, w, cos_m, sin_m, META_ROWS, META_ROWS, META_ROWS)
    o_mla = _mla_attention(q_mla, k_mla, vt_mla, km_meta, vtm_meta, tq_mla, tk_mla)
    o_sb = _sb_attention(sb_q, sb_k, sb_vt, sk_meta, svt_meta, tq_sb)

    h1, h1p, logits_t = _post(x.reshape(t, d), o_mla.reshape(t, MLA_W), o_sb.reshape(t, SB_W),
                              row(mla_out_g[0]), row(sb_out_g[0]), w_o[0].astype(BF16),
                              row(ln1_g[0]), row(ln1_b[0]), w_router[0].T, b_router[0].reshape(-1, 1), tm)

    top_i, gates, rank, cnt = _route(logits_t, tt)
    counts = cnt[:, 0].astype(I32)
    padded = (counts + ROW_BLOCK - 1) // ROW_BLOCK * ROW_BLOCK
    p_end = jnp.cumsum(padded)
    p_start = p_end - padded
    expert_ids = jnp.arange(N_EXPERTS, dtype=I32)[:, None, None]
    dest = jnp.sum(jnp.where(top_i[None] == expert_ids, p_start[:, None, None], 0), axis=0) + rank
    n_blk = (t * TOP_K + N_EXPERTS * (ROW_BLOCK - 1) + ROW_BLOCK - 1) // ROW_BLOCK
    blk_e = jnp.minimum(jnp.sum(jnp.arange(n_blk)[:, None] * ROW_BLOCK >= p_end[None, :], axis=1),
                        N_EXPERTS - 1).astype(I32)
    n_used = (p_end[-1:] // ROW_BLOCK).astype(I32)

    blk_start = jnp.arange(n_blk, dtype=I32) * ROW_BLOCK
    row_end = jnp.sum(jnp.where(blk_e[None, :] == expert_ids[:, :, 0], (p_start + counts)[:, None], 0), axis=0)
    n_valid = jnp.clip(row_end - blk_start, 0, ROW_BLOCK).astype(I32)

    n_rows = n_blk * ROW_BLOCK
    dest2 = jnp.concatenate([dest, dest + n_rows], axis=1)
    bufp = _sc_scatter_rows(h1p.reshape(2 * t, ROW_WORDS), dest2, 2 * n_rows).reshape(2, n_rows, ROW_WORDS)
    yp = _experts(blk_e, n_used, n_valid, bufp,
                  w_gate[0], b_gate[0].reshape(N_EXPERTS, 1, D_FF),
                  w_up[0], b_up[0].reshape(N_EXPERTS, 1, D_FF),
                  w_down[0], b_down[0].reshape(N_EXPERTS, 1, d))
    src2 = jnp.stack([dest, dest + n_rows]).reshape(-1)
    ygp = _sc_gather_rows(yp.reshape(2 * n_rows, ROW_WORDS), src2).reshape(2, TOP_K, t, ROW_WORDS)

    out = _combine(h1, ygp, gates.T, row(ln2_g[0]), row(ln2_b[0]), tm)
    return out.reshape(b, seq, d)
```

```python
import functools

import jax
import jax.numpy as jnp
from jax import lax
from jax.experimental import pallas as pl
from jax.experimental.pallas import tpu as pltpu
from jax.experimental.pallas import tpu_sc as plsc

F32 = jnp.float32
BF16 = jnp.bfloat16
I32 = jnp.int32
U32 = jnp.uint32

D_MODEL = 1024
N_META = 16
MLA_HEADS = 4
QK_NOPE = 128
QK_ROPE = 64
V_DIM = 128
Q_LORA = 256
KV_LORA = 256
ROPE_THETA = 10000.0
SB_HEADS = 8
SB_DIM = 64
MLA_W = MLA_HEADS * V_DIM
SB_W = SB_HEADS * SB_DIM
N_EXPERTS = 32
TOP_K = 4
D_FF = 1024
SWIGLU_LIMIT = 7.0
SWIGLU_ALPHA = 1.702
DEPTH = 1
DN_ALPHA = (2 * DEPTH) ** 0.25
LN_EPS = 1e-5
RMS_EPS = 1e-6
NEG_INF = -1e30

LANES = 128
META_ROWS = 128
QK_PAD = 2 * LANES
ROW_BLOCK = 512
LOG2_E = 1.4426950408889634
ROW_WORDS = D_MODEL // 4
MLA_GROUP = 4
SB_GROUP = 2
SC_WINDOW = 128
VMEM_LIMIT = 56 * 1024 * 1024
SB_DEAD = -105.0


def _cparams(sem):
    return pltpu.CompilerParams(dimension_semantics=sem, vmem_limit_bytes=VMEM_LIMIT)


def _rms(x, g):
    return x * lax.rsqrt(jnp.mean(x * x, axis=-1, keepdims=True) + RMS_EPS) * g


def _layer_norm(x, g, b):
    mu = jnp.mean(x, axis=-1, keepdims=True)
    xc = x - mu
    var = jnp.mean(xc * xc, axis=-1, keepdims=True)
    return xc * lax.rsqrt(var + LN_EPS) * g + b


def _dot(a, b):
    return jnp.dot(a, b, preferred_element_type=F32)


def _dot_nt(a, b):
    return lax.dot_general(a, b, (((1,), (1,)), ((), ())), preferred_element_type=F32)


def _bf16_bits(x):
    bits = pltpu.bitcast(x, U32)
    return (bits + jnp.uint32(0x7FFF) + ((bits >> 16) & jnp.uint32(1))) & jnp.uint32(0xFFFF0000)


def _pack_rows(x):
    n = x.shape[1] // 2
    return _bf16_bits(x[:, n:]) | (_bf16_bits(x[:, :n]) >> 16)


def _unpack_rows(p):
    lo = pltpu.bitcast(p << 16, F32)
    hi = pltpu.bitcast(p & jnp.uint32(0xFFFF0000), F32)
    return lo.astype(BF16), hi.astype(BF16)


def _store_halves(ref, packed):
    ref[0] = packed[:, :ROW_WORDS]
    ref[1] = packed[:, ROW_WORDS:]


def _unpack_halves(ref_halves):
    lo0, hi0 = _unpack_rows(ref_halves[0])
    lo1, hi1 = _unpack_rows(ref_halves[1])
    return lo0, lo1, hi0, hi1


def _proj_kernel(x_ref, w1_ref, wsv_ref, qg_ref, kvg_ref, wq_ref, wk_ref, wv_ref, cos_ref, sin_ref,
                 qm_ref, km_ref, vmt_ref, sq_ref, sk_ref, svt_ref, *, tm, tkm, tks):
    xb = x_ref[0].astype(BF16)
    p = _dot(xb, w1_ref[...])
    sq_ref[0] = p[:, 512:1024].astype(BF16)
    sk_ref[0] = p[:, 1024:1536].astype(BF16)
    svt = _dot_nt(wsv_ref[...], xb).astype(BF16)
    for j in range(tm // tks):
        svt_ref[0, j] = svt[:, j * tks:(j + 1) * tks]
    cos = cos_ref[...]
    sin = sin_ref[...]
    k_rope = (p[:, 1536:1664] * cos + p[:, 1664:1792] * sin).astype(BF16)
    nq = _rms(p[:, 0:256], qg_ref[...]).astype(BF16)
    nkv = _rms(p[:, 256:512], kvg_ref[...]).astype(BF16)
    qq = _dot(nq, wq_ref[...])
    kn = _dot(nkv, wk_ref[...])
    vmt = _dot_nt(wv_ref[...], nkv).astype(BF16)
    for j in range(tm // tkm):
        vmt_ref[0, j] = vmt[:, j * tkm:(j + 1) * tkm]
    scale = (QK_NOPE + QK_ROPE) ** -0.5 * LOG2_E
    for h in range(MLA_HEADS):
        lo = h * QK_PAD
        qm_ref[0, :, lo:lo + LANES] = (qq[:, h * LANES:(h + 1) * LANES] * scale).astype(BF16)
        q_rope = qq[:, 512 + h * LANES:512 + (h + 1) * LANES] * cos + qq[:, 1024 + h * LANES:1024 + (h + 1) * LANES] * sin
        qm_ref[0, :, lo + LANES:lo + QK_PAD] = (q_rope * scale).astype(BF16)
        km_ref[0, :, lo:lo + LANES] = kn[:, h * LANES:(h + 1) * LANES].astype(BF16)
        km_ref[0, :, lo + LANES:lo + QK_PAD] = k_rope


def _project(x, w, cos, sin, tm, tkm, tks):
    b, l, d = x.shape
    full = lambda a: pl.BlockSpec(a.shape, lambda bi, i: (0,) * a.ndim)
    tok = lambda width: pl.BlockSpec((1, tm, width), lambda bi, i: (bi, i, 0))
    tab = pl.BlockSpec((tm, LANES), lambda bi, i: (i, 0))
    vt = lambda width, tk: pl.BlockSpec((1, tm // tk, width, tk), lambda bi, i: (bi, i, 0, 0))
    qk_w = MLA_HEADS * QK_PAD
    return pl.pallas_call(
        functools.partial(_proj_kernel, tm=tm, tkm=tkm, tks=tks),
        grid=(b, l // tm),
        in_specs=[tok(d)] + [full(a) for a in w] + [tab, tab],
        out_specs=[tok(qk_w), tok(qk_w), vt(MLA_W, tkm), tok(SB_W), tok(SB_W), vt(SB_W, tks)],
        out_shape=[jax.ShapeDtypeStruct((b, l, qk_w), BF16), jax.ShapeDtypeStruct((b, l, qk_w), BF16),
                   jax.ShapeDtypeStruct((b, l // tkm, MLA_W, tkm), BF16),
                   jax.ShapeDtypeStruct((b, l, SB_W), BF16), jax.ShapeDtypeStruct((b, l, SB_W), BF16),
                   jax.ShapeDtypeStruct((b, l // tks, SB_W, tks), BF16)],
        compiler_params=_cparams(("parallel", "parallel")),
        name="proj",
    )(x, *w, cos, sin)


def _mla_kernel(q_ref, k_ref, vt_ref, kmeta_ref, vtmeta_ref, o_ref, m_sc, l_sc, acc_sc, *, tq, tk):
    qi = pl.program_id(2)
    heads = range(MLA_GROUP)
    q = [q_ref[0, :, h * QK_PAD:(h + 1) * QK_PAD] for h in heads]

    for h in heads:
        s = _dot_nt(kmeta_ref[0, :, h * QK_PAD:(h + 1) * QK_PAD], q[h])
        key = lax.broadcasted_iota(I32, s.shape, 0)
        s = jnp.where(key < N_META, s, NEG_INF)
        m0 = jnp.max(s, axis=0, keepdims=True)
        p0 = jnp.exp2(s - m0)
        m_sc[h] = m0
        l_sc[h] = jnp.sum(p0, axis=0, keepdims=True)
        acc_sc[h] = _dot(vtmeta_ref[0, 0, h * V_DIM:(h + 1) * V_DIM, :], p0.astype(BF16))

    def update(s, ki):
        m_prev = [m_sc[h] for h in heads]
        m_new = [jnp.maximum(m_prev[h], jnp.max(s[h], axis=0, keepdims=True)) for h in heads]
        p = [jnp.exp2(s[h] - m_new[h]) for h in heads]
        for h in heads:
            alpha = jnp.exp2(m_prev[h] - m_new[h])
            l_sc[h] = alpha * l_sc[h] + jnp.sum(p[h], axis=0, keepdims=True)
            acc_sc[h] = alpha * acc_sc[h] + _dot(vt_ref[0, ki, h * V_DIM:(h + 1) * V_DIM, :], p[h].astype(BF16))
            m_sc[h] = m_new[h]

    def scores(ki):
        start = pl.multiple_of(ki * tk, tk)
        return tuple(_dot_nt(k_ref[0, pl.ds(start, tk), h * QK_PAD:(h + 1) * QK_PAD], q[h]) for h in heads)

    def body(ki, carry):
        update(scores(ki), ki)
        return carry

    n_diag = tq // tk
    first_diag = qi * n_diag
    lax.fori_loop(0, first_diag, body, 0)
    s = scores(first_diag)
    key = lax.broadcasted_iota(I32, (tk, tq), 0)
    qry = lax.broadcasted_iota(I32, (tk, tq), 1)
    for j in range(n_diag):
        s_next = scores(first_diag + j + 1) if j + 1 < n_diag else None
        update([jnp.where(key + j * tk <= qry, s[h], NEG_INF) for h in heads], first_diag + j)
        s = s_next
    for h in heads:
        o_ref[0, :, h * V_DIM:(h + 1) * V_DIM] = (acc_sc[h] / l_sc[h]).T


def _mla_attention(q, k, vt, kmeta, vtmeta, tq, tk):
    b, l, _ = q.shape
    g = MLA_GROUP
    return pl.pallas_call(
        functools.partial(_mla_kernel, tq=tq, tk=tk),
        grid=(b, MLA_HEADS // g, l // tq),
        in_specs=[
            pl.BlockSpec((1, tq, g * QK_PAD), lambda bi, h, i: (bi, i, h)),
            pl.BlockSpec((1, l, g * QK_PAD), lambda bi, h, i: (bi, 0, h), pipeline_mode=pl.Buffered(1)),
            pl.BlockSpec((1, l // tk, g * V_DIM, tk), lambda bi, h, i: (bi, 0, h, 0), pipeline_mode=pl.Buffered(1)),
            pl.BlockSpec((1, META_ROWS, g * QK_PAD), lambda bi, h, i: (0, 0, h)),
            pl.BlockSpec((1, 1, g * V_DIM, META_ROWS), lambda bi, h, i: (0, 0, h, 0)),
        ],
        out_specs=pl.BlockSpec((1, tq, g * V_DIM), lambda bi, h, i: (bi, i, h)),
        out_shape=jax.ShapeDtypeStruct((b, l, MLA_W), F32),
        scratch_shapes=[pltpu.VMEM((g, 1, tq), F32), pltpu.VMEM((g, 1, tq), F32), pltpu.VMEM((g, V_DIM, tq), F32)],
        compiler_params=_cparams(("parallel", "parallel", "arbitrary")),
        name="mla",
    )(q, k, vt, kmeta, vtmeta)


def _sb_kernel(q_ref, k_ref, vt_ref, kmeta_ref, vtmeta_ref, o_ref, c_sc, acc_sc, *, tq):
    qi = pl.program_id(2)
    chains = [(p, h) for p in range(SB_GROUP) for h in range(2)]
    lane = lax.broadcasted_iota(I32, (tq, LANES), 1)
    q_heads = []
    for p, h in chains:
        q = q_ref[0, :, p * LANES:(p + 1) * LANES]
        keep = lane < SB_DIM if h == 0 else lane >= SB_DIM
        q_heads.append(jnp.where(keep, q, jnp.zeros_like(q)))

    def later_matrix(n):
        r = lax.broadcasted_iota(I32, (n, n), 0)
        c = lax.broadcasted_iota(I32, (n, n), 1)
        return jnp.where(c > r, 1.0, 0.0).astype(BF16)

    c_sc[...] = jnp.zeros_like(c_sc)
    acc_sc[...] = jnp.zeros_like(acc_sc)

    def block(kblk, vt, valid, u):
        z = [_dot_nt(kblk(p), q_heads[n]) for n, (p, h) in enumerate(chains)]
        log_beta, log_keep = [], []
        for n in range(len(chains)):
            softplus = jnp.maximum(z[n], 0.0) + jnp.log(1.0 + jnp.exp(-jnp.abs(z[n])))
            lk = -softplus
            if valid is not None:
                lk = jnp.where(valid, lk, 0.0)
            log_keep.append(lk)
            log_beta.append(z[n] - softplus)
        later = []
        for n in range(len(chains)):
            hi = log_keep[n].astype(BF16)
            lo = (log_keep[n] - hi.astype(F32)).astype(BF16)
            later.append(_dot(u, hi) + _dot(u, lo) + c_sc[n])
        for n, (p, h) in enumerate(chains):
            w = jnp.exp(log_beta[n] + later[n])
            if valid is not None:
                w = jnp.where(valid, w, 0.0)
            acc_sc[n] += _dot(vt(p), w.astype(BF16))
            c_sc[n] = later[n][0:1, :] + log_keep[n][0:1, :]

    def keys_at(ki):
        st = pl.multiple_of(ki * tq, tq)
        return (lambda p: k_ref[0, pl.ds(st, tq), p * LANES:(p + 1) * LANES],
                lambda p: vt_ref[0, ki, p * LANES:(p + 1) * LANES, :])

    u_blk = later_matrix(tq)
    key = lax.broadcasted_iota(I32, (tq, tq), 0)
    qry = lax.broadcasted_iota(I32, (tq, tq), 1)
    block(*keys_at(qi), key < qry, u_blk)

    def cond(carry):
        i, cm = carry
        return jnp.logical_and(i < qi, cm > SB_DEAD)

    def body(carry):
        i, _ = carry
        block(*keys_at(qi - 1 - i), None, u_blk)
        return i + 1, jnp.max(c_sc[...])

    _, cm = lax.while_loop(cond, body, (jnp.int32(0), jnp.max(c_sc[...])))

    @pl.when(cm > SB_DEAD)
    def _():
        mkey = lax.broadcasted_iota(I32, (META_ROWS, tq), 0)
        block(lambda p: kmeta_ref[0, :, p * LANES:(p + 1) * LANES],
              lambda p: vtmeta_ref[0, 0, p * LANES:(p + 1) * LANES, :], mkey < N_META, later_matrix(META_ROWS))

    feat = lax.broadcasted_iota(I32, (LANES, tq), 0)
    for p in range(SB_GROUP):
        o_ref[0, :, p * LANES:(p + 1) * LANES] = jnp.where(feat < SB_DIM, acc_sc[2 * p], acc_sc[2 * p + 1]).T


def _sb_attention(q, k, vt, kmeta, vtmeta, tq):
    b, l, _ = q.shape
    g = SB_GROUP
    gw = g * LANES
    return pl.pallas_call(
        functools.partial(_sb_kernel, tq=tq),
        grid=(b, SB_W // gw, l // tq),
        in_specs=[
            pl.BlockSpec((1, tq, gw), lambda bi, h, i: (bi, i, h)),
            pl.BlockSpec((1, l, gw), lambda bi, h, i: (bi, 0, h), pipeline_mode=pl.Buffered(1)),
            pl.BlockSpec((1, l // tq, gw, tq), lambda bi, h, i: (bi, 0, h, 0), pipeline_mode=pl.Buffered(1)),
            pl.BlockSpec((1, META_ROWS, gw), lambda bi, h, i: (0, 0, h)),
            pl.BlockSpec((1, 1, gw, META_ROWS), lambda bi, h, i: (0, 0, h, 0)),
        ],
        out_specs=pl.BlockSpec((1, tq, gw), lambda bi, h, i: (bi, i, h)),
        out_shape=jax.ShapeDtypeStruct((b, l, SB_W), F32),
        scratch_shapes=[pltpu.VMEM((2 * g, 1, tq), F32), pltpu.VMEM((2 * g, LANES, tq), F32)],
        compiler_params=_cparams(("parallel", "parallel", "arbitrary")),
        name="sb",
    )(q, k, vt, kmeta, vtmeta)


def _post_kernel(x_ref, om_ref, os_ref, g1_ref, g2_ref, wo_ref, lg_ref, lb_ref, wr_ref, br_ref,
                 h1_ref, h1p_ref, lt_ref):
    n1 = _rms(om_ref[...], g1_ref[...]).astype(BF16)
    n2 = _rms(os_ref[...], g2_ref[...]).astype(BF16)
    a = _dot(n1, wo_ref[0:MLA_W, :]) + _dot(n2, wo_ref[MLA_W:MLA_W + SB_W, :])
    h1 = _layer_norm(DN_ALPHA * x_ref[...] + a, lg_ref[...], lb_ref[...])
    h1_ref[...] = h1
    _store_halves(h1p_ref, _pack_rows(h1))
    lt_ref[...] = lax.dot_general(wr_ref[...], h1, (((1,), (1,)), ((), ())),
                                  precision=lax.Precision.HIGHEST, preferred_element_type=F32) + br_ref[...]


def _post(x2, om, osb, g1, g2, wo, lg, lb, wr_t, br, tm):
    t, d = x2.shape
    full = lambda a: pl.BlockSpec(a.shape, lambda i: (0,) * a.ndim)
    tok = lambda w: pl.BlockSpec((tm, w), lambda i: (i, 0))
    return pl.pallas_call(
        _post_kernel,
        grid=(t // tm,),
        in_specs=[tok(d), tok(MLA_W), tok(SB_W), full(g1), full(g2), full(wo), full(lg), full(lb), full(wr_t), full(br)],
        out_specs=[tok(d), pl.BlockSpec((2, tm, ROW_WORDS), lambda i: (0, i, 0)),
                   pl.BlockSpec((N_EXPERTS, tm), lambda i: (0, i))],
        out_shape=[jax.ShapeDtypeStruct((t, d), F32), jax.ShapeDtypeStruct((2, t, ROW_WORDS), U32),
                   jax.ShapeDtypeStruct((N_EXPERTS, t), F32)],
        compiler_params=_cparams(("parallel",)),
        name="post",
    )(x2, om, osb, g1, g2, wo, lg, lb, wr_t, br)


def _route_kernel(lt_ref, ti_ref, gate_ref, rank_ref, cnt_ref, carry_sc, *, tt):
    @pl.when(pl.program_id(0) == 0)
    def _():
        carry_sc[...] = jnp.zeros_like(carry_sc)

    logits = lt_ref[...]
    eidx = lax.broadcasted_iota(I32, logits.shape, 0)
    sels, vals = [], []
    for k in range(TOP_K):
        mx = jnp.max(logits, axis=0, keepdims=True)
        idx = jnp.min(jnp.where(logits == mx, eidx, N_EXPERTS), axis=0, keepdims=True)
        sel = eidx == idx
        logits = jnp.where(sel, -jnp.inf, logits)
        ti_ref[k:k + 1, :] = idx
        sels.append(sel)
        vals.append(mx)
    exps = [jnp.exp(v - vals[0]) for v in vals]
    denom = exps[0] + exps[1] + exps[2] + exps[3]
    for k in range(TOP_K):
        gate_ref[k:k + 1, :] = exps[k] / denom

    chosen = sum(jnp.where(s, 1.0, 0.0) for s in sels)
    r = lax.broadcasted_iota(I32, (tt, tt), 0)
    c = lax.broadcasted_iota(I32, (tt, tt), 1)
    before = jnp.where(r < c, 1.0, 0.0).astype(BF16)
    rank = _dot(chosen.astype(BF16), before) + carry_sc[...]
    for k in range(TOP_K):
        rank_ref[k:k + 1, :] = jnp.sum(jnp.where(sels[k], rank, 0.0), axis=0, keepdims=True).astype(I32)
    carry_sc[...] += jnp.sum(chosen, axis=1, keepdims=True)
    cnt_ref[...] = jnp.broadcast_to(carry_sc[...], cnt_ref.shape)


def _route(lt, tt):
    e, t = lt.shape
    tok = pl.BlockSpec((TOP_K, tt), lambda i: (0, i))
    return pl.pallas_call(
        functools.partial(_route_kernel, tt=tt),
        grid=(t // tt,),
        in_specs=[pl.BlockSpec((e, tt), lambda i: (0, i))],
        out_specs=[tok, tok, tok, pl.BlockSpec((e, LANES), lambda i: (0, 0))],
        out_shape=[jax.ShapeDtypeStruct((TOP_K, t), I32), jax.ShapeDtypeStruct((TOP_K, t), F32),
                   jax.ShapeDtypeStruct((TOP_K, t), I32), jax.ShapeDtypeStruct((e, LANES), F32)],
        scratch_shapes=[pltpu.VMEM((e, 1), F32)],
        compiler_params=_cparams(("arbitrary",)),
        name="route",
    )(lt)


def _sc_mesh():
    return plsc.VectorSubcoreMesh(core_axis_name="c", subcore_axis_name="s")


def _sc_gather_rows(x, idx):
    m = idx.shape[0]
    w = x.shape[1]

    @pl.kernel(out_type=jax.ShapeDtypeStruct((m, w), x.dtype), mesh=_sc_mesh())
    def gather(x_hbm, i_hbm, o_hbm):
        def body(i_vmem, o_vmem):
            pltpu.sync_copy(x_hbm.at[i_vmem.at[0]], o_vmem)

        pltpu.emit_pipeline(
            body, grid=(m // SC_WINDOW,),
            in_specs=[pl.BlockSpec((1, SC_WINDOW), lambda i: (0, i))],
            out_specs=[pl.BlockSpec((SC_WINDOW, w), lambda i: (i, 0))],
            core_axis_name=("c", "s"), dimension_semantics=(pltpu.PARALLEL,),
        )(i_hbm, o_hbm)

    return gather(x, idx.reshape(1, m))


def _sc_scatter_rows(x, dest, n_rows):
    kk, t = dest.shape
    w = x.shape[1]
    nt = t // SC_WINDOW

    @pl.kernel(out_type=jax.ShapeDtypeStruct((n_rows, w), x.dtype), mesh=_sc_mesh(), scratch_types=[])
    def scatter(x_hbm, i_hbm, o_hbm):
        def body(x_vmem, i_vmem):
            pltpu.sync_copy(x_vmem, o_hbm.at[i_vmem.at[0]])

        pltpu.emit_pipeline(
            body, grid=(kk * nt,),
            in_specs=[pl.BlockSpec((SC_WINDOW, w), lambda i: (i % nt, 0)),
                      pl.BlockSpec((1, SC_WINDOW), lambda i: (i // nt, i % nt))],
            out_specs=[],
            core_axis_name=("c", "s"), dimension_semantics=(pltpu.PARALLEL,),
        )(x_hbm, i_hbm)

    return scatter(x, dest)


def _expert_kernel(be_ref, nu_ref, nv_ref, x_ref, wg_ref, bg_ref, wu_ref, bu_ref, wd_ref, bd_ref, y_ref,
                   wg_sc, wu_sc, wd_sc):
    i = pl.program_id(0)
    active = i < nu_ref[0]

    @pl.when(jnp.logical_and(active, jnp.logical_or(i == 0, be_ref[i] != be_ref[jnp.maximum(i - 1, 0)])))
    def _():
        wg_sc[...] = wg_ref[0].astype(BF16)
        wu_sc[...] = wu_ref[0].astype(BF16)
        wd_sc[...] = wd_ref[0].astype(BF16)

    @pl.when(active)
    def _():
        live = lax.broadcasted_iota(I32, (ROW_BLOCK, ROW_WORDS), 0) < nv_ref[i]
        zero = jnp.zeros((ROW_BLOCK, ROW_WORDS), U32)
        xs = _unpack_halves((jnp.where(live, x_ref[0], zero), jnp.where(live, x_ref[1], zero)))
        cw = D_MODEL // len(xs)

        def proj(w_sc, b_ref):
            acc = b_ref[0]
            for j, xj in enumerate(xs):
                acc = acc + _dot(xj, w_sc[j * cw:(j + 1) * cw, :])
            return acc

        g = jnp.minimum(proj(wg_sc, bg_ref), SWIGLU_LIMIT)
        u = jnp.clip(proj(wu_sc, bu_ref), -SWIGLU_LIMIT, SWIGLU_LIMIT)
        a = (u + 1.0) * (g * (1.0 / (1.0 + jnp.exp(-SWIGLU_ALPHA * g))))
        _store_halves(y_ref, _pack_rows(_dot(a.astype(BF16), wd_sc[...]) + bd_ref[0]))

    @pl.when(i >= nu_ref[0])
    def _():
        y_ref[...] = jnp.zeros_like(y_ref)


def _experts(blk_e, n_used, n_valid, buf, wg, bg, wu, bu, wd, bd):
    _, n_rows, d = buf.shape
    n_blk = n_rows // ROW_BLOCK
    wspec = lambda a: pl.BlockSpec((1,) + a.shape[1:], lambda i, be, nu, nv: (be[i], 0, 0))
    rows = pl.BlockSpec((2, ROW_BLOCK, d), lambda i, be, nu, nv: (0, i, 0))
    return pl.pallas_call(
        _expert_kernel,
        grid_spec=pltpu.PrefetchScalarGridSpec(
            num_scalar_prefetch=3,
            grid=(n_blk,),
            in_specs=[rows, wspec(wg), wspec(bg), wspec(wu), wspec(bu), wspec(wd), wspec(bd)],
            out_specs=rows,
            scratch_shapes=[pltpu.VMEM(wg.shape[1:], BF16), pltpu.VMEM(wu.shape[1:], BF16),
                            pltpu.VMEM(wd.shape[1:], BF16)],
        ),
        out_shape=jax.ShapeDtypeStruct((2, n_rows, d), U32),
        compiler_params=_cparams(("arbitrary",)),
        name="experts",
    )(blk_e, n_used, n_valid, buf, wg, bg, wu, bu, wd, bd)


def _combine_kernel(h1_ref, y_ref, gate_ref, lg_ref, lb_ref, o_ref):
    gates = gate_ref[...]
    parts = None
    for k in range(TOP_K):
        g = gates[:, k:k + 1]
        ys = [g * c.astype(F32) for c in _unpack_halves((y_ref[0, k], y_ref[1, k]))]
        parts = ys if parts is None else [p + y for p, y in zip(parts, ys)]
    f = jnp.concatenate(parts, axis=1)
    o_ref[...] = _layer_norm(DN_ALPHA * h1_ref[...] + f, lg_ref[...], lb_ref[...])


def _combine(h1, yg, gates, lg, lb, tm):
    t, d = h1.shape
    full = lambda a: pl.BlockSpec(a.shape, lambda i: (0,) * a.ndim)
    return pl.pallas_call(
        _combine_kernel,
        grid=(t // tm,),
        in_specs=[pl.BlockSpec((tm, d), lambda i: (i, 0)),
                  pl.BlockSpec((2, TOP_K, tm, ROW_WORDS), lambda i: (0, 0, i, 0)),
                  pl.BlockSpec((tm, TOP_K), lambda i: (i, 0)), full(lg), full(lb)],
        out_specs=pl.BlockSpec((tm, d), lambda i: (i, 0)),
        out_shape=jax.ShapeDtypeStruct((t, d), F32),
        compiler_params=_cparams(("parallel",)),
        name="combine",
    )(h1, yg, gates, lg, lb)


def _rope_tables(positions):
    half = QK_ROPE // 2
    freqs = ROPE_THETA ** (-jnp.arange(half, dtype=F32) * 2.0 / QK_ROPE)
    ang = positions.astype(F32)[:, None] * freqs[None, :]
    pad = jnp.zeros((positions.shape[0], LANES - QK_ROPE), F32)
    cos, sin = jnp.cos(ang), jnp.sin(ang)
    return jnp.concatenate([cos, cos, pad], axis=1), jnp.concatenate([sin, sin, pad], axis=1)


def _rope_cols(w):
    half = QK_ROPE // 2
    x1, x2 = w[:, :half], w[:, half:]
    pad = jnp.zeros((w.shape[0], LANES - QK_ROPE), w.dtype)
    return jnp.concatenate([x1, x2, pad], axis=1), jnp.concatenate([-x2, x1, pad], axis=1)


def _layout_weights(w_in, q_norm_g, w_uq, kv_norm_g, w_ukv):
    c_q = w_in[:, 0:Q_LORA]
    c_kv = w_in[:, Q_LORA:Q_LORA + KV_LORA]
    o = Q_LORA + KV_LORA
    k_r = w_in[:, o:o + QK_ROPE]
    o += QK_ROPE
    sb_q = w_in[:, o:o + SB_W] * (SB_DIM ** -0.5)
    sb_k = w_in[:, o + SB_W:o + 2 * SB_W]
    sb_v = w_in[:, o + 2 * SB_W:o + 3 * SB_W]
    kr, kr_rot = _rope_cols(k_r)
    w1 = jnp.concatenate([c_q, c_kv, sb_q, sb_k, kr, kr_rot], axis=1).astype(BF16)
    wsv_t = sb_v.T.astype(BF16)

    uq = w_uq.reshape(Q_LORA, MLA_HEADS, QK_NOPE + QK_ROPE)
    nope = uq[:, :, :QK_NOPE].reshape(Q_LORA, MLA_HEADS * QK_NOPE)
    ropes = [_rope_cols(uq[:, h, QK_NOPE:]) for h in range(MLA_HEADS)]
    wq = jnp.concatenate([nope] + [r[0] for r in ropes] + [r[1] for r in ropes], axis=1).astype(BF16)

    ukv = w_ukv.reshape(KV_LORA, MLA_HEADS, QK_NOPE + V_DIM)
    wk = ukv[:, :, :QK_NOPE].reshape(KV_LORA, -1).astype(BF16)
    wv_t = ukv[:, :, QK_NOPE:].reshape(KV_LORA, -1).T.astype(BF16)
    return (w1, wsv_t, q_norm_g.reshape(1, -1), kv_norm_g.reshape(1, -1), wq, wk, wv_t)


def kernel(x, meta_tokens, w_in, q_norm_g, w_uq, kv_norm_g, w_ukv, mla_out_g, sb_out_g, w_o, ln1_g, ln1_b,
           w_router, b_router, w_gate, b_gate, w_up, b_up, w_down, b_down, ln2_g, ln2_b):
    b, seq, d = x.shape
    t = b * seq
    tq_mla = min(512, seq)
    tk_mla = min(512, seq)
    tq_sb = min(256, seq)
    tm_proj = min(512, seq)
    tm = min(256, seq)
    tt = min(512, t)
    row = lambda v: v.reshape(1, -1)

    w = _layout_weights(w_in[0], q_norm_g[0], w_uq[0], kv_norm_g[0], w_ukv[0])
    cos_x, sin_x = _rope_tables(N_META + jnp.arange(seq))
    cos_m, sin_m = _rope_tables(jnp.arange(META_ROWS))
    meta = jnp.zeros((1, META_ROWS, d), x.dtype).at[0, :N_META].set(meta_tokens.astype(x.dtype))
    q_mla, k_mla, vt_mla, sb_q, sb_k, sb_vt = _project(x, w, cos_x, sin_x, tm_proj, tk_mla, tq_sb)
    _, km_meta, vtm_meta, _, sk_meta, svt_meta = _project(meta, w, cos_m, sin_m, META_ROWS, META_ROWS, META_ROWS)
    o_mla = _mla_attention(q_mla, k_mla, vt_mla, km_meta, vtm_meta, tq_mla, tk_mla)
    o_sb = _sb_attention(sb_q, sb_k, sb_vt, sk_meta, svt_meta, tq_sb)

    h1, h1p, logits_t = _post(x.reshape(t, d), o_mla.reshape(t, MLA_W), o_sb.reshape(t, SB_W),
                              row(mla_out_g[0]), row(sb_out_g[0]), w_o[0].astype(BF16),
                              row(ln1_g[0]), row(ln1_b[0]), w_router[0].T, b_router[0].reshape(-1, 1), tm)

    top_i, gates, rank, cnt = _route(logits_t, tt)
    counts = cnt[:, 0].astype(I32)
    padded = (counts + ROW_BLOCK - 1) // ROW_BLOCK * ROW_BLOCK
    p_end = jnp.cumsum(padded)
    p_start = p_end - padded
    expert_ids = jnp.arange(N_EXPERTS, dtype=I32)[:, None, None]
    dest = jnp.sum(jnp.where(top_i[None] == expert_ids, p_start[:, None, None], 0), axis=0) + rank
    n_blk = (t * TOP_K + N_EXPERTS * (ROW_BLOCK - 1) + ROW_BLOCK - 1) // ROW_BLOCK
    blk_e = jnp.minimum(jnp.sum(jnp.arange(n_blk)[:, None] * ROW_BLOCK >= p_end[None, :], axis=1),
                        N_EXPERTS - 1).astype(I32)
    n_used = (p_end[-1:] // ROW_BLOCK).astype(I32)

    blk_start = jnp.arange(n_blk, dtype=I32) * ROW_BLOCK
    row_end = jnp.sum(jnp.where(blk_e[None, :] == expert_ids[:, :, 0], (p_start + counts)[:, None], 0), axis=0)
    n_valid = jnp.clip(row_end - blk_start, 0, ROW_BLOCK).astype(I32)

    n_rows = n_blk * ROW_BLOCK
    dest2 = jnp.concatenate([dest, dest + n_rows], axis=1)
    bufp = _sc_scatter_rows(h1p.reshape(2 * t, ROW_WORDS), dest2, 2 * n_rows).reshape(2, n_rows, ROW_WORDS)
    yp = _experts(blk_e, n_used, n_valid, bufp,
                  w_gate[0], b_gate[0].reshape(N_EXPERTS, 1, D_FF),
                  w_up[0], b_up[0].reshape(N_EXPERTS, 1, D_FF),
                  w_down[0], b_down[0].reshape(N_EXPERTS, 1, d))
    src2 = jnp.stack([dest, dest + n_rows]).reshape(-1)
    ygp = _sc_gather_rows(yp.reshape(2 * n_rows, ROW_WORDS), src2).reshape(2, TOP_K, t, ROW_WORDS)

    out = _combine(h1, ygp, gates.T, row(ln2_g[0]), row(ln2_b[0]), tm)
    return out.reshape(b, seq, d)
```

```python
import functools

import jax
import jax.numpy as jnp
from jax import lax
from jax.experimental import pallas as pl
from jax.experimental.pallas import tpu as pltpu
from jax.experimental.pallas import tpu_sc as plsc

F32 = jnp.float32
BF16 = jnp.bfloat16
I32 = jnp.int32
U32 = jnp.uint32

D_MODEL = 1024
N_META = 16
MLA_HEADS = 4
QK_NOPE = 128
QK_ROPE = 64
V_DIM = 128
Q_LORA = 256
KV_LORA = 256
ROPE_THETA = 10000.0
SB_HEADS = 8
SB_DIM = 64
MLA_W = MLA_HEADS * V_DIM
SB_W = SB_HEADS * SB_DIM
N_EXPERTS = 32
TOP_K = 4
D_FF = 1024
SWIGLU_LIMIT = 7.0
SWIGLU_ALPHA = 1.702
DEPTH = 1
DN_ALPHA = (2 * DEPTH) ** 0.25
LN_EPS = 1e-5
RMS_EPS = 1e-6
NEG_INF = -1e30

LANES = 128
META_ROWS = 128
QK_PAD = 2 * LANES
ROW_BLOCK = 512
LOG2_E = 1.4426950408889634
ROW_WORDS = D_MODEL // 4
MLA_GROUP = 4
SB_GROUP = 4
SC_WINDOW = 128
VMEM_LIMIT = 56 * 1024 * 1024
SB_DEAD = -105.0


def _cparams(sem):
    return pltpu.CompilerParams(dimension_semantics=sem, vmem_limit_bytes=VMEM_LIMIT)


def _rms(x, g):
    return x * lax.rsqrt(jnp.mean(x * x, axis=-1, keepdims=True) + RMS_EPS) * g


def _layer_norm(x, g, b):
    mu = jnp.mean(x, axis=-1, keepdims=True)
    xc = x - mu
    var = jnp.mean(xc * xc, axis=-1, keepdims=True)
    return xc * lax.rsqrt(var + LN_EPS) * g + b


def _dot(a, b):
    return jnp.dot(a, b, preferred_element_type=F32)


def _dot_nt(a, b):
    return lax.dot_general(a, b, (((1,), (1,)), ((), ())), preferred_element_type=F32)


def _bf16_bits(x):
    bits = pltpu.bitcast(x, U32)
    return (bits + jnp.uint32(0x7FFF) + ((bits >> 16) & jnp.uint32(1))) & jnp.uint32(0xFFFF0000)


def _pack_rows(x):
    n = x.shape[1] // 2
    return _bf16_bits(x[:, n:]) | (_bf16_bits(x[:, :n]) >> 16)


def _unpack_rows(p):
    lo = pltpu.bitcast(p << 16, F32)
    hi = pltpu.bitcast(p & jnp.uint32(0xFFFF0000), F32)
    return lo.astype(BF16), hi.astype(BF16)


def _store_halves(ref, packed):
    ref[0] = packed[:, :ROW_WORDS]
    ref[1] = packed[:, ROW_WORDS:]


def _unpack_halves(ref_halves):
    lo0, hi0 = _unpack_rows(ref_halves[0])
    lo1, hi1 = _unpack_rows(ref_halves[1])
    return lo0, lo1, hi0, hi1


def _proj_kernel(x_ref, w1_ref, wsv_ref, qg_ref, kvg_ref, wq_ref, wk_ref, wv_ref, cos_ref, sin_ref,
                 qm_ref, km_ref, vmt_ref, sq_ref, sk_ref, svt_ref, *, tm, tkm, tks):
    xb = x_ref[0].astype(BF16)
    p = _dot(xb, w1_ref[...])
    sq_ref[0] = p[:, 512:1024].astype(BF16)
    sk_ref[0] = p[:, 1024:1536].astype(BF16)
    svt = _dot_nt(wsv_ref[...], xb).astype(BF16)
    for j in range(tm // tks):
        svt_ref[0, j] = svt[:, j * tks:(j + 1) * tks]
    cos = cos_ref[...]
    sin = sin_ref[...]
    k_rope = (p[:, 1536:1664] * cos + p[:, 1664:1792] * sin).astype(BF16)
    nq = _rms(p[:, 0:256], qg_ref[...]).astype(BF16)
    nkv = _rms(p[:, 256:512], kvg_ref[...]).astype(BF16)
    qq = _dot(nq, wq_ref[...])
    kn = _dot(nkv, wk_ref[...])
    vmt = _dot_nt(wv_ref[...], nkv).astype(BF16)
    for j in range(tm // tkm):
        vmt_ref[0, j] = vmt[:, j * tkm:(j + 1) * tkm]
    scale = (QK_NOPE + QK_ROPE) ** -0.5 * LOG2_E
    for h in range(MLA_HEADS):
        lo = h * QK_PAD
        qm_ref[0, :, lo:lo + LANES] = (qq[:, h * LANES:(h + 1) * LANES] * scale).astype(BF16)
        q_rope = qq[:, 512 + h * LANES:512 + (h + 1) * LANES] * cos + qq[:, 1024 + h * LANES:1024 + (h + 1) * LANES] * sin
        qm_ref[0, :, lo + LANES:lo + QK_PAD] = (q_rope * scale).astype(BF16)
        km_ref[0, :, lo:lo + LANES] = kn[:, h * LANES:(h + 1) * LANES].astype(BF16)
        km_ref[0, :, lo + LANES:lo + QK_PAD] = k_rope


def _project(x, w, cos, sin, tm, tkm, tks):
    b, l, d = x.shape
    full = lambda a: pl.BlockSpec(a.shape, lambda bi, i: (0,) * a.ndim)
    tok = lambda width: pl.BlockSpec((1, tm, width), lambda bi, i: (bi, i, 0))
    tab = pl.BlockSpec((tm, LANES), lambda bi, i: (i, 0))
    vt = lambda width, tk: pl.BlockSpec((1, tm // tk, width, tk), lambda bi, i: (bi, i, 0, 0))
    qk_w = MLA_HEADS * QK_PAD
    return pl.pallas_call(
        functools.partial(_proj_kernel, tm=tm, tkm=tkm, tks=tks),
        grid=(b, l // tm),
        in_specs=[tok(d)] + [full(a) for a in w] + [tab, tab],
        out_specs=[tok(qk_w), tok(qk_w), vt(MLA_W, tkm), tok(SB_W), tok(SB_W), vt(SB_W, tks)],
        out_shape=[jax.ShapeDtypeStruct((b, l, qk_w), BF16), jax.ShapeDtypeStruct((b, l, qk_w), BF16),
                   jax.ShapeDtypeStruct((b, l // tkm, MLA_W, tkm), BF16),
                   jax.ShapeDtypeStruct((b, l, SB_W), BF16), jax.ShapeDtypeStruct((b, l, SB_W), BF16),
                   jax.ShapeDtypeStruct((b, l // tks, SB_W, tks), BF16)],
        compiler_params=_cparams(("parallel", "parallel")),
        name="proj",
    )(x, *w, cos, sin)


def _mla_kernel(q_ref, k_ref, vt_ref, kmeta_ref, vtmeta_ref, o_ref, m_sc, l_sc, acc_sc, *, tq, tk):
    qi = pl.program_id(2)
    heads = range(MLA_GROUP)
    q = [q_ref[0, :, h * QK_PAD:(h + 1) * QK_PAD] for h in heads]

    for h in heads:
        s = _dot_nt(kmeta_ref[0, :, h * QK_PAD:(h + 1) * QK_PAD], q[h])
        key = lax.broadcasted_iota(I32, s.shape, 0)
        s = jnp.where(key < N_META, s, NEG_INF)
        m0 = jnp.max(s, axis=0, keepdims=True)
        p0 = jnp.exp2(s - m0)
        m_sc[h] = m0
        l_sc[h] = jnp.sum(p0, axis=0, keepdims=True)
        acc_sc[h] = _dot(vtmeta_ref[0, 0, h * V_DIM:(h + 1) * V_DIM, :], p0.astype(BF16))

    def update(s, ki):
        m_prev = [m_sc[h] for h in heads]
        m_new = [jnp.maximum(m_prev[h], jnp.max(s[h], axis=0, keepdims=True)) for h in heads]
        p = [jnp.exp2(s[h] - m_new[h]) for h in heads]
        for h in heads:
            alpha = jnp.exp2(m_prev[h] - m_new[h])
            l_sc[h] = alpha * l_sc[h] + jnp.sum(p[h], axis=0, keepdims=True)
            acc_sc[h] = alpha * acc_sc[h] + _dot(vt_ref[0, ki, h * V_DIM:(h + 1) * V_DIM, :], p[h].astype(BF16))
            m_sc[h] = m_new[h]

    def scores(ki):
        start = pl.multiple_of(ki * tk, tk)
        return tuple(_dot_nt(k_ref[0, pl.ds(start, tk), h * QK_PAD:(h + 1) * QK_PAD], q[h]) for h in heads)

    def body(ki, carry):
        update(scores(ki), ki)
        return carry

    n_diag = tq // tk
    first_diag = qi * n_diag
    lax.fori_loop(0, first_diag, body, 0)
    s = scores(first_diag)
    key = lax.broadcasted_iota(I32, (tk, tq), 0)
    qry = lax.broadcasted_iota(I32, (tk, tq), 1)
    for j in range(n_diag):
        s_next = scores(first_diag + j + 1) if j + 1 < n_diag else None
        update([jnp.where(key + j * tk <= qry, s[h], NEG_INF) for h in heads], first_diag + j)
        s = s_next
    for h in heads:
        o_ref[0, :, h * V_DIM:(h + 1) * V_DIM] = (acc_sc[h] / l_sc[h]).T


def _mla_attention(q, k, vt, kmeta, vtmeta, tq, tk):
    b, l, _ = q.shape
    g = MLA_GROUP
    return pl.pallas_call(
        functools.partial(_mla_kernel, tq=tq, tk=tk),
        grid=(b, MLA_HEADS // g, l // tq),
        in_specs=[
            pl.BlockSpec((1, tq, g * QK_PAD), lambda bi, h, i: (bi, i, h)),
            pl.BlockSpec((1, l, g * QK_PAD), lambda bi, h, i: (bi, 0, h), pipeline_mode=pl.Buffered(1)),
            pl.BlockSpec((1, l // tk, g * V_DIM, tk), lambda bi, h, i: (bi, 0, h, 0), pipeline_mode=pl.Buffered(1)),
            pl.BlockSpec((1, META_ROWS, g * QK_PAD), lambda bi, h, i: (0, 0, h)),
            pl.BlockSpec((1, 1, g * V_DIM, META_ROWS), lambda bi, h, i: (0, 0, h, 0)),
        ],
        out_specs=pl.BlockSpec((1, tq, g * V_DIM), lambda bi, h, i: (bi, i, h)),
        out_shape=jax.ShapeDtypeStruct((b, l, MLA_W), F32),
        scratch_shapes=[pltpu.VMEM((g, 1, tq), F32), pltpu.VMEM((g, 1, tq), F32), pltpu.VMEM((g, V_DIM, tq), F32)],
        compiler_params=_cparams(("parallel", "parallel", "arbitrary")),
        name="mla",
    )(q, k, vt, kmeta, vtmeta)


def _sb_kernel(q_ref, k_ref, vt_ref, kmeta_ref, vtmeta_ref, o_ref, c_sc, acc_sc, *, tq):
    qi = pl.program_id(2)
    chains = [(p, h) for p in range(SB_GROUP) for h in range(2)]
    lane = lax.broadcasted_iota(I32, (tq, LANES), 1)
    q_heads = []
    for p, h in chains:
        q = q_ref[0, :, p * LANES:(p + 1) * LANES]
        keep = lane < SB_DIM if h == 0 else lane >= SB_DIM
        q_heads.append(jnp.where(keep, q, jnp.zeros_like(q)))

    def later_matrix(n):
        r = lax.broadcasted_iota(I32, (n, n), 0)
        c = lax.broadcasted_iota(I32, (n, n), 1)
        return jnp.where(c > r, 1.0, 0.0).astype(BF16)

    c_sc[...] = jnp.zeros_like(c_sc)
    acc_sc[...] = jnp.zeros_like(acc_sc)

    def block(kblk, vt, valid, u):
        z = [_dot_nt(kblk(p), q_heads[n]) for n, (p, h) in enumerate(chains)]
        log_beta, log_keep = [], []
        for n in range(len(chains)):
            softplus = jnp.maximum(z[n], 0.0) + jnp.log(1.0 + jnp.exp(-jnp.abs(z[n])))
            lk = -softplus
            if valid is not None:
                lk = jnp.where(valid, lk, 0.0)
            log_keep.append(lk)
            log_beta.append(z[n] - softplus)
        later = []
        for n in range(len(chains)):
            hi = log_keep[n].astype(BF16)
            lo = (log_keep[n] - hi.astype(F32)).astype(BF16)
            later.append(_dot(u, hi) + _dot(u, lo) + c_sc[n])
        for n, (p, h) in enumerate(chains):
            w = jnp.exp(log_beta[n] + later[n])
            if valid is not None:
                w = jnp.where(valid, w, 0.0)
            acc_sc[n] += _dot(vt(p), w.astype(BF16))
            c_sc[n] = later[n][0:1, :] + log_keep[n][0:1, :]

    def keys_at(ki):
        st = pl.multiple_of(ki * tq, tq)
        return (lambda p: k_ref[0, pl.ds(st, tq), p * LANES:(p + 1) * LANES],
                lambda p: vt_ref[0, ki, p * LANES:(p + 1) * LANES, :])

    u_blk = later_matrix(tq)
    key = lax.broadcasted_iota(I32, (tq, tq), 0)
    qry = lax.broadcasted_iota(I32, (tq, tq), 1)
    block(*keys_at(qi), key < qry, u_blk)

    def cond(carry):
        i, cm = carry
        return jnp.logical_and(i < qi, cm > SB_DEAD)

    def body(carry):
        i, _ = carry
        block(*keys_at(qi - 1 - i), None, u_blk)
        return i + 1, jnp.max(c_sc[...])

    _, cm = lax.while_loop(cond, body, (jnp.int32(0), jnp.max(c_sc[...])))

    @pl.when(cm > SB_DEAD)
    def _():
        mkey = lax.broadcasted_iota(I32, (META_ROWS, tq), 0)
        block(lambda p: kmeta_ref[0, :, p * LANES:(p + 1) * LANES],
              lambda p: vtmeta_ref[0, 0, p * LANES:(p + 1) * LANES, :], mkey < N_META, later_matrix(META_ROWS))

    feat = lax.broadcasted_iota(I32, (LANES, tq), 0)
    for p in range(SB_GROUP):
        o_ref[0, :, p * LANES:(p + 1) * LANES] = jnp.where(feat < SB_DIM, acc_sc[2 * p], acc_sc[2 * p + 1]).T


def _sb_attention(q, k, vt, kmeta, vtmeta, tq):
    b, l, _ = q.shape
    g = SB_GROUP
    gw = g * LANES
    return pl.pallas_call(
        functools.partial(_sb_kernel, tq=tq),
        grid=(b, SB_W // gw, l // tq),
        in_specs=[
            pl.BlockSpec((1, tq, gw), lambda bi, h, i: (bi, i, h)),
            pl.BlockSpec((1, l, gw), lambda bi, h, i: (bi, 0, h), pipeline_mode=pl.Buffered(1)),
            pl.BlockSpec((1, l // tq, gw, tq), lambda bi, h, i: (bi, 0, h, 0), pipeline_mode=pl.Buffered(1)),
            pl.BlockSpec((1, META_ROWS, gw), lambda bi, h, i: (0, 0, h)),
            pl.BlockSpec((1, 1, gw, META_ROWS), lambda bi, h, i: (0, 0, h, 0)),
        ],
        out_specs=pl.BlockSpec((1, tq, gw), lambda bi, h, i: (bi, i, h)),
        out_shape=jax.ShapeDtypeStruct((b, l, SB_W), F32),
        scratch_shapes=[pltpu.VMEM((2 * g, 1, tq), F32), pltpu.VMEM((2 * g, LANES, tq), F32)],
        compiler_params=_cparams(("parallel", "parallel", "arbitrary")),
        name="sb",
    )(q, k, vt, kmeta, vtmeta)


def _post_kernel(x_ref, om_ref, os_ref, g1_ref, g2_ref, wo_ref, lg_ref, lb_ref, wr_ref, br_ref,
                 h1_ref, h1p_ref, lt_ref):
    n1 = _rms(om_ref[...], g1_ref[...]).astype(BF16)
    n2 = _rms(os_ref[...], g2_ref[...]).astype(BF16)
    a = _dot(n1, wo_ref[0:MLA_W, :]) + _dot(n2, wo_ref[MLA_W:MLA_W + SB_W, :])
    h1 = _layer_norm(DN_ALPHA * x_ref[...] + a, lg_ref[...], lb_ref[...])
    h1_ref[...] = h1
    _store_halves(h1p_ref, _pack_rows(h1))
    lt_ref[...] = lax.dot_general(wr_ref[...], h1, (((1,), (1,)), ((), ())),
                                  precision=lax.Precision.HIGHEST, preferred_element_type=F32) + br_ref[...]


def _post(x2, om, osb, g1, g2, wo, lg, lb, wr_t, br, tm):
    t, d = x2.shape
    full = lambda a: pl.BlockSpec(a.shape, lambda i: (0,) * a.ndim)
    tok = lambda w: pl.BlockSpec((tm, w), lambda i: (i, 0))
    return pl.pallas_call(
        _post_kernel,
        grid=(t // tm,),
        in_specs=[tok(d), tok(MLA_W), tok(SB_W), full(g1), full(g2), full(wo), full(lg), full(lb), full(wr_t), full(br)],
        out_specs=[tok(d), pl.BlockSpec((2, tm, ROW_WORDS), lambda i: (0, i, 0)),
                   pl.BlockSpec((N_EXPERTS, tm), lambda i: (0, i))],
        out_shape=[jax.ShapeDtypeStruct((t, d), F32), jax.ShapeDtypeStruct((2, t, ROW_WORDS), U32),
                   jax.ShapeDtypeStruct((N_EXPERTS, t), F32)],
        compiler_params=_cparams(("parallel",)),
        name="post",
    )(x2, om, osb, g1, g2, wo, lg, lb, wr_t, br)


def _route_kernel(lt_ref, ti_ref, gate_ref, rank_ref, cnt_ref, carry_sc, *, tt):
    @pl.when(pl.program_id(0) == 0)
    def _():
        carry_sc[...] = jnp.zeros_like(carry_sc)

    logits = lt_ref[...]
    eidx = lax.broadcasted_iota(I32, logits.shape, 0)
    sels, vals = [], []
    for k in range(TOP_K):
        mx = jnp.max(logits, axis=0, keepdims=True)
        idx = jnp.min(jnp.where(logits == mx, eidx, N_EXPERTS), axis=0, keepdims=True)
        sel = eidx == idx
        logits = jnp.where(sel, -jnp.inf, logits)
        ti_ref[k:k + 1, :] = idx
        sels.append(sel)
        vals.append(mx)
    exps = [jnp.exp(v - vals[0]) for v in vals]
    denom = exps[0] + exps[1] + exps[2] + exps[3]
    for k in range(TOP_K):
        gate_ref[k:k + 1, :] = exps[k] / denom

    chosen = sum(jnp.where(s, 1.0, 0.0) for s in sels)
    r = lax.broadcasted_iota(I32, (tt, tt), 0)
    c = lax.broadcasted_iota(I32, (tt, tt), 1)
    before = jnp.where(r < c, 1.0, 0.0).astype(BF16)
    rank = _dot(chosen.astype(BF16), before) + carry_sc[...]
    for k in range(TOP_K):
        rank_ref[k:k + 1, :] = jnp.sum(jnp.where(sels[k], rank, 0.0), axis=0, keepdims=True).astype(I32)
    carry_sc[...] += jnp.sum(chosen, axis=1, keepdims=True)
    cnt_ref[...] = jnp.broadcast_to(carry_sc[...], cnt_ref.shape)


def _route(lt, tt):
    e, t = lt.shape
    tok = pl.BlockSpec((TOP_K, tt), lambda i: (0, i))
    return pl.pallas_call(
        functools.partial(_route_kernel, tt=tt),
        grid=(t // tt,),
        in_specs=[pl.BlockSpec((e, tt), lambda i: (0, i))],
        out_specs=[tok, tok, tok, pl.BlockSpec((e, LANES), lambda i: (0, 0))],
        out_shape=[jax.ShapeDtypeStruct((TOP_K, t), I32), jax.ShapeDtypeStruct((TOP_K, t), F32),
                   jax.ShapeDtypeStruct((TOP_K, t), I32), jax.ShapeDtypeStruct((e, LANES), F32)],
        scratch_shapes=[pltpu.VMEM((e, 1), F32)],
        compiler_params=_cparams(("arbitrary",)),
        name="route",
    )(lt)


def _sc_mesh():
    return plsc.VectorSubcoreMesh(core_axis_name="c", subcore_axis_name="s")


def _sc_gather_rows(x, idx):
    m = idx.shape[0]
    w = x.shape[1]

    @pl.kernel(out_type=jax.ShapeDtypeStruct((m, w), x.dtype), mesh=_sc_mesh())
    def gather(x_hbm, i_hbm, o_hbm):
        def body(i_vmem, o_vmem):
            pltpu.sync_copy(x_hbm.at[i_vmem.at[0]], o_vmem)

        pltpu.emit_pipeline(
            body, grid=(m // SC_WINDOW,),
            in_specs=[pl.BlockSpec((1, SC_WINDOW), lambda i: (0, i))],
            out_specs=[pl.BlockSpec((SC_WINDOW, w), lambda i: (i, 0))],
            core_axis_name=("c", "s"), dimension_semantics=(pltpu.PARALLEL,),
        )(i_hbm, o_hbm)

    return gather(x, idx.reshape(1, m))


def _sc_scatter_rows(x, dest, n_rows):
    kk, t = dest.shape
    w = x.shape[1]
    nt = t // SC_WINDOW

    @pl.kernel(out_type=jax.ShapeDtypeStruct((n_rows, w), x.dtype), mesh=_sc_mesh(), scratch_types=[])
    def scatter(x_hbm, i_hbm, o_hbm):
        def body(x_vmem, i_vmem):
            pltpu.sync_copy(x_vmem, o_hbm.at[i_vmem.at[0]])

        pltpu.emit_pipeline(
            body, grid=(kk * nt,),
            in_specs=[pl.BlockSpec((SC_WINDOW, w), lambda i: (i % nt, 0)),
                      pl.BlockSpec((1, SC_WINDOW), lambda i: (i // nt, i % nt))],
            out_specs=[],
            core_axis_name=("c", "s"), dimension_semantics=(pltpu.PARALLEL,),
        )(x_hbm, i_hbm)

    return scatter(x, dest)


def _expert_kernel(be_ref, nu_ref, nv_ref, x_ref, wg_ref, bg_ref, wu_ref, bu_ref, wd_ref, bd_ref, y_ref,
                   wg_sc, wu_sc, wd_sc):
    i = pl.program_id(0)
    active = i < nu_ref[0]

    @pl.when(jnp.logical_and(active, jnp.logical_or(i == 0, be_ref[i] != be_ref[jnp.maximum(i - 1, 0)])))
    def _():
        wg_sc[...] = wg_ref[0].astype(BF16)
        wu_sc[...] = wu_ref[0].astype(BF16)
        wd_sc[...] = wd_ref[0].astype(BF16)

    @pl.when(active)
    def _():
        live = lax.broadcasted_iota(I32, (ROW_BLOCK, ROW_WORDS), 0) < nv_ref[i]
        zero = jnp.zeros((ROW_BLOCK, ROW_WORDS), U32)
        xs = _unpack_halves((jnp.where(live, x_ref[0], zero), jnp.where(live, x_ref[1], zero)))
        cw = D_MODEL // len(xs)

        def proj(w_sc, b_ref):
            acc = b_ref[0]
            for j, xj in enumerate(xs):
                acc = acc + _dot(xj, w_sc[j * cw:(j + 1) * cw, :])
            return acc

        g = jnp.minimum(proj(wg_sc, bg_ref), SWIGLU_LIMIT)
        u = jnp.clip(proj(wu_sc, bu_ref), -SWIGLU_LIMIT, SWIGLU_LIMIT)
        a = (u + 1.0) * (g * (1.0 / (1.0 + jnp.exp(-SWIGLU_ALPHA * g))))
        _store_halves(y_ref, _pack_rows(_dot(a.astype(BF16), wd_sc[...]) + bd_ref[0]))

    @pl.when(i >= nu_ref[0])
    def _():
        y_ref[...] = jnp.zeros_like(y_ref)


def _experts(blk_e, n_used, n_valid, buf, wg, bg, wu, bu, wd, bd):
    _, n_rows, d = buf.shape
    n_blk = n_rows // ROW_BLOCK
    wspec = lambda a: pl.BlockSpec((1,) + a.shape[1:], lambda i, be, nu, nv: (be[i], 0, 0))
    rows = pl.BlockSpec((2, ROW_BLOCK, d), lambda i, be, nu, nv: (0, i, 0))
    return pl.pallas_call(
        _expert_kernel,
        grid_spec=pltpu.PrefetchScalarGridSpec(
            num_scalar_prefetch=3,
            grid=(n_blk,),
            in_specs=[rows, wspec(wg), wspec(bg), wspec(wu), wspec(bu), wspec(wd), wspec(bd)],
            out_specs=rows,
            scratch_shapes=[pltpu.VMEM(wg.shape[1:], BF16), pltpu.VMEM(wu.shape[1:], BF16),
                            pltpu.VMEM(wd.shape[1:], BF16)],
        ),
        out_shape=jax.ShapeDtypeStruct((2, n_rows, d), U32),
        compiler_params=_cparams(("arbitrary",)),
        name="experts",
    )(blk_e, n_used, n_valid, buf, wg, bg, wu, bu, wd, bd)


def _combine_kernel(h1_ref, y_ref, gate_ref, lg_ref, lb_ref, o_ref):
    gates = gate_ref[...]
    parts = None
    for k in range(TOP_K):
        g = gates[:, k:k + 1]
        ys = [g * c.astype(F32) for c in _unpack_halves((y_ref[0, k], y_ref[1, k]))]
        parts = ys if parts is None else [p + y for p, y in zip(parts, ys)]
    f = jnp.concatenate(parts, axis=1)
    o_ref[...] = _layer_norm(DN_ALPHA * h1_ref[...] + f, lg_ref[...], lb_ref[...])


def _combine(h1, yg, gates, lg, lb, tm):
    t, d = h1.shape
    full = lambda a: pl.BlockSpec(a.shape, lambda i: (0,) * a.ndim)
    return pl.pallas_call(
        _combine_kernel,
        grid=(t // tm,),
        in_specs=[pl.BlockSpec((tm, d), lambda i: (i, 0)),
                  pl.BlockSpec((2, TOP_K, tm, ROW_WORDS), lambda i: (0, 0, i, 0)),
                  pl.BlockSpec((tm, TOP_K), lambda i: (i, 0)), full(lg), full(lb)],
        out_specs=pl.BlockSpec((tm, d), lambda i: (i, 0)),
        out_shape=jax.ShapeDtypeStruct((t, d), F32),
        compiler_params=_cparams(("parallel",)),
        name="combine",
    )(h1, yg, gates, lg, lb)


def _rope_tables(positions):
    half = QK_ROPE // 2
    freqs = ROPE_THETA ** (-jnp.arange(half, dtype=F32) * 2.0 / QK_ROPE)
    ang = positions.astype(F32)[:, None] * freqs[None, :]
    pad = jnp.zeros((positions.shape[0], LANES - QK_ROPE), F32)
    cos, sin = jnp.cos(ang), jnp.sin(ang)
    return jnp.concatenate([cos, cos, pad], axis=1), jnp.concatenate([sin, sin, pad], axis=1)


def _rope_cols(w):
    half = QK_ROPE // 2
    x1, x2 = w[:, :half], w[:, half:]
    pad = jnp.zeros((w.shape[0], LANES - QK_ROPE), w.dtype)
    return jnp.concatenate([x1, x2, pad], axis=1), jnp.concatenate([-x2, x1, pad], axis=1)


def _layout_weights(w_in, q_norm_g, w_uq, kv_norm_g, w_ukv):
    c_q = w_in[:, 0:Q_LORA]
    c_kv = w_in[:, Q_LORA:Q_LORA + KV_LORA]
    o = Q_LORA + KV_LORA
    k_r = w_in[:, o:o + QK_ROPE]
    o += QK_ROPE
    sb_q = w_in[:, o:o + SB_W] * (SB_DIM ** -0.5)
    sb_k = w_in[:, o + SB_W:o + 2 * SB_W]
    sb_v = w_in[:, o + 2 * SB_W:o + 3 * SB_W]
    kr, kr_rot = _rope_cols(k_r)
    w1 = jnp.concatenate([c_q, c_kv, sb_q, sb_k, kr, kr_rot], axis=1).astype(BF16)
    wsv_t = sb_v.T.astype(BF16)

    uq = w_uq.reshape(Q_LORA, MLA_HEADS, QK_NOPE + QK_ROPE)
    nope = uq[:, :, :QK_NOPE].reshape(Q_LORA, MLA_HEADS * QK_NOPE)
    ropes = [_rope_cols(uq[:, h, QK_NOPE:]) for h in range(MLA_HEADS)]
    wq = jnp.concatenate([nope] + [r[0] for r in ropes] + [r[1] for r in ropes], axis=1).astype(BF16)

    ukv = w_ukv.reshape(KV_LORA, MLA_HEADS, QK_NOPE + V_DIM)
    wk = ukv[:, :, :QK_NOPE].reshape(KV_LORA, -1).astype(BF16)
    wv_t = ukv[:, :, QK_NOPE:].reshape(KV_LORA, -1).T.astype(BF16)
    return (w1, wsv_t, q_norm_g.reshape(1, -1), kv_norm_g.reshape(1, -1), wq, wk, wv_t)


def kernel(x, meta_tokens, w_in, q_norm_g, w_uq, kv_norm_g, w_ukv, mla_out_g, sb_out_g, w_o, ln1_g, ln1_b,
           w_router, b_router, w_gate, b_gate, w_up, b_up, w_down, b_down, ln2_g, ln2_b):
    b, seq, d = x.shape
    t = b * seq
    tq_mla = min(512, seq)
    tk_mla = min(512, seq)
    tq_sb = min(256, seq)
    tm_proj = min(512, seq)
    tm = min(512, seq)
    tt = min(512, t)
    row = lambda v: v.reshape(1, -1)

    w = _layout_weights(w_in[0], q_norm_g[0], w_uq[0], kv_norm_g[0], w_ukv[0])
    cos_x, sin_x = _rope_tables(N_META + jnp.arange(seq))
    cos_m, sin_m = _rope_tables(jnp.arange(META_ROWS))
    meta = jnp.zeros((1, META_ROWS, d), x.dtype).at[0, :N_META].set(meta_tokens.astype(x.dtype))
    q_mla, k_mla, vt_mla, sb_q, sb_k, sb_vt = _project(x, w, cos_x, sin_x, tm_proj, tk_mla, tq_sb)
    _, km_meta, vtm_meta, _, sk_meta, svt_meta = _project(meta, w, cos_m, sin_m, META_ROWS, META_ROWS, META_ROWS)
    o_mla = _mla_attention(q_mla, k_mla, vt_mla, km_meta, vtm_meta, tq_mla, tk_mla)
    o_sb = _sb_attention(sb_q, sb_k, sb_vt, sk_meta, svt_meta, tq_sb)

    h1, h1p, logits_t = _post(x.reshape(t, d), o_mla.reshape(t, MLA_W), o_sb.reshape(t, SB_W),
                              row(mla_out_g[0]), row(sb_out_g[0]), w_o[0].astype(BF16),
                              row(ln1_g[0]), row(ln1_b[0]), w_router[0].T, b_router[0].reshape(-1, 1), tm)

    top_i, gates, rank, cnt = _route(logits_t, tt)
    counts = cnt[:, 0].astype(I32)
    padded = (counts + ROW_BLOCK - 1) // ROW_BLOCK * ROW_BLOCK
    p_end = jnp.cumsum(padded)
    p_start = p_end - padded
    expert_ids = jnp.arange(N_EXPERTS, dtype=I32)[:, None, None]
    dest = jnp.sum(jnp.where(top_i[None] == expert_ids, p_start[:, None, None], 0), axis=0) + rank
    n_blk = (t * TOP_K + N_EXPERTS * (ROW_BLOCK - 1) + ROW_BLOCK - 1) // ROW_BLOCK
    blk_e = jnp.minimum(jnp.sum(jnp.arange(n_blk)[:, None] * ROW_BLOCK >= p_end[None, :], axis=1),
                        N_EXPERTS - 1).astype(I32)
    n_used = (p_end[-1:] // ROW_BLOCK).astype(I32)

    blk_start = jnp.arange(n_blk, dtype=I32) * ROW_BLOCK
    row_end = jnp.sum(jnp.where(blk_e[None, :] == expert_ids[:, :, 0], (p_start + counts)[:, None], 0), axis=0)
    n_valid = jnp.clip(row_end - blk_start, 0, ROW_BLOCK).astype(I32)

    n_rows = n_blk * ROW_BLOCK
    dest2 = jnp.concatenate([dest, dest + n_rows], axis=1)
    bufp = _sc_scatter_rows(h1p.reshape(2 * t, ROW_WORDS), dest2, 2 * n_rows).reshape(2, n_rows, ROW_WORDS)
    yp = _experts(blk_e, n_used, n_valid, bufp,
                  w_gate[0], b_gate[0].reshape(N_EXPERTS, 1, D_FF),
                  w_up[0], b_up[0].reshape(N_EXPERTS, 1, D_FF),
                  w_down[0], b_down[0].reshape(N_EXPERTS, 1, d))
    src2 = jnp.stack([dest, dest + n_rows]).reshape(-1)
    ygp = _sc_gather_rows(yp.reshape(2 * n_rows, ROW_WORDS), src2).reshape(2, TOP_K, t, ROW_WORDS)

    out = _combine(h1, ygp, gates.T, row(ln2_g[0]), row(ln2_b[0]), tm)
    return out.reshape(b, seq, d)
```

```python
import functools

import jax
import jax.numpy as jnp
from jax import lax
from jax.experimental import pallas as pl
from jax.experimental.pallas import tpu as pltpu
from jax.experimental.pallas import tpu_sc as plsc

F32 = jnp.float32
BF16 = jnp.bfloat16
I32 = jnp.int32
U32 = jnp.uint32

D_MODEL = 1024
N_META = 16
MLA_HEADS = 4
QK_NOPE = 128
QK_ROPE = 64
V_DIM = 128
Q_LORA = 256
KV_LORA = 256
ROPE_THETA = 10000.0
SB_HEADS = 8
SB_DIM = 64
MLA_W = MLA_HEADS * V_DIM
SB_W = SB_HEADS * SB_DIM
N_EXPERTS = 32
TOP_K = 4
D_FF = 1024
SWIGLU_LIMIT = 7.0
SWIGLU_ALPHA = 1.702
DEPTH = 1
DN_ALPHA = (2 * DEPTH) ** 0.25
LN_EPS = 1e-5
RMS_EPS = 1e-6
NEG_INF = -1e30

LANES = 128
META_ROWS = 128
QK_PAD = 2 * LANES
ROW_BLOCK = 512
LOG2_E = 1.4426950408889634
ROW_WORDS = D_MODEL // 4
MLA_GROUP = 4
SB_GROUP = 4
SC_WINDOW = 128
VMEM_LIMIT = 56 * 1024 * 1024
SB_DEAD = -105.0


def _cparams(sem):
    return pltpu.CompilerParams(dimension_semantics=sem, vmem_limit_bytes=VMEM_LIMIT)


def _rms(x, g):
    return x * lax.rsqrt(jnp.mean(x * x, axis=-1, keepdims=True) + RMS_EPS) * g


def _layer_norm(x, g, b):
    mu = jnp.mean(x, axis=-1, keepdims=True)
    xc = x - mu
    var = jnp.mean(xc * xc, axis=-1, keepdims=True)
    return xc * lax.rsqrt(var + LN_EPS) * g + b


def _dot(a, b):
    return jnp.dot(a, b, preferred_element_type=F32)


def _dot_nt(a, b):
    return lax.dot_general(a, b, (((1,), (1,)), ((), ())), preferred_element_type=F32)


def _bf16_bits(x):
    bits = pltpu.bitcast(x, U32)
    return (bits + jnp.uint32(0x7FFF) + ((bits >> 16) & jnp.uint32(1))) & jnp.uint32(0xFFFF0000)


def _pack_rows(x):
    n = x.shape[1] // 2
    return _bf16_bits(x[:, n:]) | (_bf16_bits(x[:, :n]) >> 16)


def _unpack_rows(p):
    lo = pltpu.bitcast(p << 16, F32)
    hi = pltpu.bitcast(p & jnp.uint32(0xFFFF0000), F32)
    return lo.astype(BF16), hi.astype(BF16)


def _store_halves(ref, packed):
    ref[0] = packed[:, :ROW_WORDS]
    ref[1] = packed[:, ROW_WORDS:]


def _unpack_halves(ref_halves):
    lo0, hi0 = _unpack_rows(ref_halves[0])
    lo1, hi1 = _unpack_rows(ref_halves[1])
    return lo0, lo1, hi0, hi1


def _proj_kernel(x_ref, w1_ref, wsv_ref, qg_ref, kvg_ref, wq_ref, wk_ref, wv_ref, cos_ref, sin_ref,
                 qm_ref, km_ref, vmt_ref, sq_ref, sk_ref, svt_ref, *, tm, tkm, tks):
    xb = x_ref[0].astype(BF16)
    p = _dot(xb, w1_ref[...])
    sq_ref[0] = p[:, 512:1024].astype(BF16)
    sk_ref[0] = p[:, 1024:1536].astype(BF16)
    svt = _dot_nt(wsv_ref[...], xb).astype(BF16)
    for j in range(tm // tks):
        svt_ref[0, j] = svt[:, j * tks:(j + 1) * tks]
    cos = cos_ref[...]
    sin = sin_ref[...]
    k_rope = (p[:, 1536:1664] * cos + p[:, 1664:1792] * sin).astype(BF16)
    nq = _rms(p[:, 0:256], qg_ref[...]).astype(BF16)
    nkv = _rms(p[:, 256:512], kvg_ref[...]).astype(BF16)
    qq = _dot(nq, wq_ref[...])
    kn = _dot(nkv, wk_ref[...])
    vmt = _dot_nt(wv_ref[...], nkv).astype(BF16)
    for j in range(tm // tkm):
        vmt_ref[0, j] = vmt[:, j * tkm:(j + 1) * tkm]
    scale = (QK_NOPE + QK_ROPE) ** -0.5 * LOG2_E
    for h in range(MLA_HEADS):
        lo = h * QK_PAD
        qm_ref[0, :, lo:lo + LANES] = (qq[:, h * LANES:(h + 1) * LANES] * scale).astype(BF16)
        q_rope = qq[:, 512 + h * LANES:512 + (h + 1) * LANES] * cos + qq[:, 1024 + h * LANES:1024 + (h + 1) * LANES] * sin
        qm_ref[0, :, lo + LANES:lo + QK_PAD] = (q_rope * scale).astype(BF16)
        km_ref[0, :, lo:lo + LANES] = kn[:, h * LANES:(h + 1) * LANES].astype(BF16)
        km_ref[0, :, lo + LANES:lo + QK_PAD] = k_rope


def _project(x, w, cos, sin, tm, tkm, tks):
    b, l, d = x.shape
    full = lambda a: pl.BlockSpec(a.shape, lambda bi, i: (0,) * a.ndim)
    tok = lambda width: pl.BlockSpec((1, tm, width), lambda bi, i: (bi, i, 0))
    tab = pl.BlockSpec((tm, LANES), lambda bi, i: (i, 0))
    vt = lambda width, tk: pl.BlockSpec((1, tm // tk, width, tk), lambda bi, i: (bi, i, 0, 0))
    qk_w = MLA_HEADS * QK_PAD
    return pl.pallas_call(
        functools.partial(_proj_kernel, tm=tm, tkm=tkm, tks=tks),
        grid=(b, l // tm),
        in_specs=[tok(d)] + [full(a) for a in w] + [tab, tab],
        out_specs=[tok(qk_w), tok(qk_w), vt(MLA_W, tkm), tok(SB_W), tok(SB_W), vt(SB_W, tks)],
        out_shape=[jax.ShapeDtypeStruct((b, l, qk_w), BF16), jax.ShapeDtypeStruct((b, l, qk_w), BF16),
                   jax.ShapeDtypeStruct((b, l // tkm, MLA_W, tkm), BF16),
                   jax.ShapeDtypeStruct((b, l, SB_W), BF16), jax.ShapeDtypeStruct((b, l, SB_W), BF16),
                   jax.ShapeDtypeStruct((b, l // tks, SB_W, tks), BF16)],
        compiler_params=_cparams(("parallel", "parallel")),
        name="proj",
    )(x, *w, cos, sin)


def _mla_kernel(q_ref, k_ref, vt_ref, kmeta_ref, vtmeta_ref, o_ref, m_sc, l_sc, acc_sc, *, tq, tk):
    qi = pl.program_id(2)
    heads = range(MLA_GROUP)
    q = [q_ref[0, :, h * QK_PAD:(h + 1) * QK_PAD] for h in heads]

    for h in heads:
        s = _dot_nt(kmeta_ref[0, :, h * QK_PAD:(h + 1) * QK_PAD], q[h])
        key = lax.broadcasted_iota(I32, s.shape, 0)
        s = jnp.where(key < N_META, s, NEG_INF)
        m0 = jnp.max(s, axis=0, keepdims=True)
        p0 = jnp.exp2(s - m0)
        m_sc[h] = m0
        l_sc[h] = jnp.sum(p0, axis=0, keepdims=True)
        acc_sc[h] = _dot(vtmeta_ref[0, 0, h * V_DIM:(h + 1) * V_DIM, :], p0.astype(BF16))

    def update(s, ki):
        m_prev = [m_sc[h] for h in heads]
        m_new = [jnp.maximum(m_prev[h], jnp.max(s[h], axis=0, keepdims=True)) for h in heads]
        p = [jnp.exp2(s[h] - m_new[h]) for h in heads]
        for h in heads:
            alpha = jnp.exp2(m_prev[h] - m_new[h])
            l_sc[h] = alpha * l_sc[h] + jnp.sum(p[h], axis=0, keepdims=True)
            acc_sc[h] = alpha * acc_sc[h] + _dot(vt_ref[0, ki, h * V_DIM:(h + 1) * V_DIM, :], p[h].astype(BF16))
            m_sc[h] = m_new[h]

    def scores(ki):
        start = pl.multiple_of(ki * tk, tk)
        return tuple(_dot_nt(k_ref[0, pl.ds(start, tk), h * QK_PAD:(h + 1) * QK_PAD], q[h]) for h in heads)

    def body(ki, carry):
        update(scores(ki), ki)
        return carry

    n_diag = tq // tk
    first_diag = qi * n_diag
    lax.fori_loop(0, first_diag, body, 0)
    s = scores(first_diag)
    key = lax.broadcasted_iota(I32, (tk, tq), 0)
    qry = lax.broadcasted_iota(I32, (tk, tq), 1)
    for j in range(n_diag):
        s_next = scores(first_diag + j + 1) if j + 1 < n_diag else None
        update([jnp.where(key + j * tk <= qry, s[h], NEG_INF) for h in heads], first_diag + j)
        s = s_next
    for h in heads:
        o_ref[0, :, h * V_DIM:(h + 1) * V_DIM] = (acc_sc[h] / l_sc[h]).T


def _mla_attention(q, k, vt, kmeta, vtmeta, tq, tk):
    b, l, _ = q.shape
    g = MLA_GROUP
    return pl.pallas_call(
        functools.partial(_mla_kernel, tq=tq, tk=tk),
        grid=(b, MLA_HEADS // g, l // tq),
        in_specs=[
            pl.BlockSpec((1, tq, g * QK_PAD), lambda bi, h, i: (bi, i, h)),
            pl.BlockSpec((1, l, g * QK_PAD), lambda bi, h, i: (bi, 0, h), pipeline_mode=pl.Buffered(1)),
            pl.BlockSpec((1, l // tk, g * V_DIM, tk), lambda bi, h, i: (bi, 0, h, 0), pipeline_mode=pl.Buffered(1)),
            pl.BlockSpec((1, META_ROWS, g * QK_PAD), lambda bi, h, i: (0, 0, h)),
            pl.BlockSpec((1, 1, g * V_DIM, META_ROWS), lambda bi, h, i: (0, 0, h, 0)),
        ],
        out_specs=pl.BlockSpec((1, tq, g * V_DIM), lambda bi, h, i: (bi, i, h)),
        out_shape=jax.ShapeDtypeStruct((b, l, MLA_W), F32),
        scratch_shapes=[pltpu.VMEM((g, 1, tq), F32), pltpu.VMEM((g, 1, tq), F32), pltpu.VMEM((g, V_DIM, tq), F32)],
        compiler_params=_cparams(("parallel", "parallel", "arbitrary")),
        name="mla",
    )(q, k, vt, kmeta, vtmeta)


def _sb_kernel(q_ref, k_ref, vt_ref, kmeta_ref, vtmeta_ref, o_ref, c_sc, acc_sc, *, tq):
    qi = pl.program_id(2)
    chains = [(p, h) for p in range(SB_GROUP) for h in range(2)]
    lane = lax.broadcasted_iota(I32, (tq, LANES), 1)
    q_heads = []
    for p, h in chains:
        q = q_ref[0, :, p * LANES:(p + 1) * LANES]
        keep = lane < SB_DIM if h == 0 else lane >= SB_DIM
        q_heads.append(jnp.where(keep, q, jnp.zeros_like(q)))

    def later_matrix(n):
        r = lax.broadcasted_iota(I32, (n, n), 0)
        c = lax.broadcasted_iota(I32, (n, n), 1)
        return jnp.where(c > r, 1.0, 0.0).astype(BF16)

    c_sc[...] = jnp.zeros_like(c_sc)
    acc_sc[...] = jnp.zeros_like(acc_sc)

    def block(kblk, vt, valid, u):
        z = [_dot_nt(kblk(p), q_heads[n]) for n, (p, h) in enumerate(chains)]
        log_beta, log_keep = [], []
        for n in range(len(chains)):
            softplus = jnp.maximum(z[n], 0.0) + jnp.log(1.0 + jnp.exp(-jnp.abs(z[n])))
            lk = -softplus
            if valid is not None:
                lk = jnp.where(valid, lk, 0.0)
            log_keep.append(lk)
            log_beta.append(z[n] - softplus)
        later = []
        for n in range(len(chains)):
            hi = log_keep[n].astype(BF16)
            lo = (log_keep[n] - hi.astype(F32)).astype(BF16)
            later.append(_dot(u, hi) + _dot(u, lo) + c_sc[n])
        for n, (p, h) in enumerate(chains):
            w = jnp.exp(log_beta[n] + later[n])
            if valid is not None:
                w = jnp.where(valid, w, 0.0)
            acc_sc[n] += _dot(vt(p), w.astype(BF16))
            c_sc[n] = later[n][0:1, :] + log_keep[n][0:1, :]

    def keys_at(ki):
        st = pl.multiple_of(ki * tq, tq)
        return (lambda p: k_ref[0, pl.ds(st, tq), p * LANES:(p + 1) * LANES],
                lambda p: vt_ref[0, ki, p * LANES:(p + 1) * LANES, :])

    u_blk = later_matrix(tq)
    key = lax.broadcasted_iota(I32, (tq, tq), 0)
    qry = lax.broadcasted_iota(I32, (tq, tq), 1)
    block(*keys_at(qi), key < qry, u_blk)

    def cond(carry):
        i, cm = carry
        return jnp.logical_and(i < qi, cm > SB_DEAD)

    def body(carry):
        i, _ = carry
        block(*keys_at(qi - 1 - i), None, u_blk)
        return i + 1, jnp.max(c_sc[...])

    _, cm = lax.while_loop(cond, body, (jnp.int32(0), jnp.max(c_sc[...])))

    @pl.when(cm > SB_DEAD)
    def _():
        mkey = lax.broadcasted_iota(I32, (META_ROWS, tq), 0)
        block(lambda p: kmeta_ref[0, :, p * LANES:(p + 1) * LANES],
              lambda p: vtmeta_ref[0, 0, p * LANES:(p + 1) * LANES, :], mkey < N_META, later_matrix(META_ROWS))

    feat = lax.broadcasted_iota(I32, (LANES, tq), 0)
    for p in range(SB_GROUP):
        o_ref[0, :, p * LANES:(p + 1) * LANES] = jnp.where(feat < SB_DIM, acc_sc[2 * p], acc_sc[2 * p + 1]).T


def _sb_attention(q, k, vt, kmeta, vtmeta, tq):
    b, l, _ = q.shape
    g = SB_GROUP
    gw = g * LANES
    return pl.pallas_call(
        functools.partial(_sb_kernel, tq=tq),
        grid=(b, SB_W // gw, l // tq),
        in_specs=[
            pl.BlockSpec((1, tq, gw), lambda bi, h, i: (bi, i, h)),
            pl.BlockSpec((1, l, gw), lambda bi, h, i: (bi, 0, h), pipeline_mode=pl.Buffered(1)),
            pl.BlockSpec((1, l // tq, gw, tq), lambda bi, h, i: (bi, 0, h, 0), pipeline_mode=pl.Buffered(1)),
            pl.BlockSpec((1, META_ROWS, gw), lambda bi, h, i: (0, 0, h)),
            pl.BlockSpec((1, 1, gw, META_ROWS), lambda bi, h, i: (0, 0, h, 0)),
        ],
        out_specs=pl.BlockSpec((1, tq, gw), lambda bi, h, i: (bi, i, h)),
        out_shape=jax.ShapeDtypeStruct((b, l, SB_W), F32),
        scratch_shapes=[pltpu.VMEM((2 * g, 1, tq), F32), pltpu.VMEM((2 * g, LANES, tq), F32)],
        compiler_params=_cparams(("parallel", "parallel", "arbitrary")),
        name="sb",
    )(q, k, vt, kmeta, vtmeta)


def _post_kernel(x_ref, om_ref, os_ref, g1_ref, g2_ref, wo_ref, lg_ref, lb_ref, wr_ref, br_ref,
                 h1_ref, h1p_ref, lt_ref):
    n1 = _rms(om_ref[...], g1_ref[...]).astype(BF16)
    n2 = _rms(os_ref[...], g2_ref[...]).astype(BF16)
    a = _dot(n1, wo_ref[0:MLA_W, :]) + _dot(n2, wo_ref[MLA_W:MLA_W + SB_W, :])
    h1 = _layer_norm(DN_ALPHA * x_ref[...] + a, lg_ref[...], lb_ref[...])
    h1_ref[...] = h1
    _store_halves(h1p_ref, _pack_rows(h1))
    lt_ref[...] = lax.dot_general(wr_ref[...], h1, (((1,), (1,)), ((), ())),
                                  precision=lax.Precision.HIGHEST, preferred_element_type=F32) + br_ref[...]


def _post(x2, om, osb, g1, g2, wo, lg, lb, wr_t, br, tm):
    t, d = x2.shape
    full = lambda a: pl.BlockSpec(a.shape, lambda i: (0,) * a.ndim)
    tok = lambda w: pl.BlockSpec((tm, w), lambda i: (i, 0))
    return pl.pallas_call(
        _post_kernel,
        grid=(t // tm,),
        in_specs=[tok(d), tok(MLA_W), tok(SB_W), full(g1), full(g2), full(wo), full(lg), full(lb), full(wr_t), full(br)],
        out_specs=[tok(d), pl.BlockSpec((2, tm, ROW_WORDS), lambda i: (0, i, 0)),
                   pl.BlockSpec((N_EXPERTS, tm), lambda i: (0, i))],
        out_shape=[jax.ShapeDtypeStruct((t, d), F32), jax.ShapeDtypeStruct((2, t, ROW_WORDS), U32),
                   jax.ShapeDtypeStruct((N_EXPERTS, t), F32)],
        compiler_params=_cparams(("parallel",)),
        name="post",
    )(x2, om, osb, g1, g2, wo, lg, lb, wr_t, br)


def _route_kernel(lt_ref, ti_ref, gate_ref, rank_ref, cnt_ref, carry_sc, *, tt):
    @pl.when(pl.program_id(0) == 0)
    def _():
        carry_sc[...] = jnp.zeros_like(carry_sc)

    logits = lt_ref[...]
    eidx = lax.broadcasted_iota(I32, logits.shape, 0)
    sels, vals = [], []
    for k in range(TOP_K):
        mx = jnp.max(logits, axis=0, keepdims=True)
        idx = jnp.min(jnp.where(logits == mx, eidx, N_EXPERTS), axis=0, keepdims=True)
        sel = eidx == idx
        logits = jnp.where(sel, -jnp.inf, logits)
        ti_ref[k:k + 1, :] = idx
        sels.append(sel)
        vals.append(mx)
    exps = [jnp.exp(v - vals[0]) for v in vals]
    denom = exps[0] + exps[1] + exps[2] + exps[3]
    for k in range(TOP_K):
        gate_ref[k:k + 1, :] = exps[k] / denom

    chosen = sum(jnp.where(s, 1.0, 0.0) for s in sels)
    r = lax.broadcasted_iota(I32, (tt, tt), 0)
    c = lax.broadcasted_iota(I32, (tt, tt), 1)
    before = jnp.where(r < c, 1.0, 0.0).astype(BF16)
    rank = _dot(chosen.astype(BF16), before) + carry_sc[...]
    for k in range(TOP_K):
        rank_ref[k:k + 1, :] = jnp.sum(jnp.where(sels[k], rank, 0.0), axis=0, keepdims=True).astype(I32)
    carry_sc[...] += jnp.sum(chosen, axis=1, keepdims=True)
    cnt_ref[...] = jnp.broadcast_to(carry_sc[...], cnt_ref.shape)


def _route(lt, tt):
    e, t = lt.shape
    tok = pl.BlockSpec((TOP_K, tt), lambda i: (0, i))
    return pl.pallas_call(
        functools.partial(_route_kernel, tt=tt),
        grid=(t // tt,),
        in_specs=[pl.BlockSpec((e, tt), lambda i: (0, i))],
        out_specs=[tok, tok, tok, pl.BlockSpec((e, LANES), lambda i: (0, 0))],
        out_shape=[jax.ShapeDtypeStruct((TOP_K, t), I32), jax.ShapeDtypeStruct((TOP_K, t), F32),
                   jax.ShapeDtypeStruct((TOP_K, t), I32), jax.ShapeDtypeStruct((e, LANES), F32)],
        scratch_shapes=[pltpu.VMEM((e, 1), F32)],
        compiler_params=_cparams(("arbitrary",)),
        name="route",
    )(lt)


def _sc_mesh():
    return plsc.VectorSubcoreMesh(core_axis_name="c", subcore_axis_name="s")


def _sc_gather_rows(x, idx):
    m = idx.shape[0]
    w = x.shape[1]

    @pl.kernel(out_type=jax.ShapeDtypeStruct((m, w), x.dtype), mesh=_sc_mesh())
    def gather(x_hbm, i_hbm, o_hbm):
        def body(i_vmem, o_vmem):
            pltpu.sync_copy(x_hbm.at[i_vmem.at[0]], o_vmem)

        pltpu.emit_pipeline(
            body, grid=(m // SC_WINDOW,),
            in_specs=[pl.BlockSpec((1, SC_WINDOW), lambda i: (0, i))],
            out_specs=[pl.BlockSpec((SC_WINDOW, w), lambda i: (i, 0))],
            core_axis_name=("c", "s"), dimension_semantics=(pltpu.PARALLEL,),
        )(i_hbm, o_hbm)

    return gather(x, idx.reshape(1, m))


def _sc_scatter_rows(x, dest, n_rows):
    kk, t = dest.shape
    w = x.shape[1]
    nt = t // SC_WINDOW

    @pl.kernel(out_type=jax.ShapeDtypeStruct((n_rows, w), x.dtype), mesh=_sc_mesh(), scratch_types=[])
    def scatter(x_hbm, i_hbm, o_hbm):
        def body(x_vmem, i_vmem):
            pltpu.sync_copy(x_vmem, o_hbm.at[i_vmem.at[0]])

        pltpu.emit_pipeline(
            body, grid=(kk * nt,),
            in_specs=[pl.BlockSpec((SC_WINDOW, w), lambda i: (i % nt, 0)),
                      pl.BlockSpec((1, SC_WINDOW), lambda i: (i // nt, i % nt))],
            out_specs=[],
            core_axis_name=("c", "s"), dimension_semantics=(pltpu.PARALLEL,),
        )(x_hbm, i_hbm)

    return scatter(x, dest)


def _expert_kernel(be_ref, nu_ref, nv_ref, x_ref, wg_ref, bg_ref, wu_ref, bu_ref, wd_ref, bd_ref, y_ref,
                   wg_sc, wu_sc, wd_sc):
    i = pl.program_id(0)
    active = i < nu_ref[0]

    @pl.when(jnp.logical_and(active, jnp.logical_or(i == 0, be_ref[i] != be_ref[jnp.maximum(i - 1, 0)])))
    def _():
        wg_sc[...] = wg_ref[0].astype(BF16)
        wu_sc[...] = wu_ref[0].astype(BF16)
        wd_sc[...] = wd_ref[0].astype(BF16)

    @pl.when(active)
    def _():
        live = lax.broadcasted_iota(I32, (ROW_BLOCK, ROW_WORDS), 0) < nv_ref[i]
        zero = jnp.zeros((ROW_BLOCK, ROW_WORDS), U32)
        xs = _unpack_halves((jnp.where(live, x_ref[0], zero), jnp.where(live, x_ref[1], zero)))
        cw = D_MODEL // len(xs)

        def proj(w_sc, b_ref):
            acc = b_ref[0]
            for j, xj in enumerate(xs):
                acc = acc + _dot(xj, w_sc[j * cw:(j + 1) * cw, :])
            return acc

        g = jnp.minimum(proj(wg_sc, bg_ref), SWIGLU_LIMIT)
        u = jnp.clip(proj(wu_sc, bu_ref), -SWIGLU_LIMIT, SWIGLU_LIMIT)
        a = (u + 1.0) * (g * (1.0 / (1.0 + jnp.exp(-SWIGLU_ALPHA * g))))
        _store_halves(y_ref, _pack_rows(_dot(a.astype(BF16), wd_sc[...]) + bd_ref[0]))

    @pl.when(i >= nu_ref[0])
    def _():
        y_ref[...] = jnp.zeros_like(y_ref)


def _experts(blk_e, n_used, n_valid, buf, wg, bg, wu, bu, wd, bd):
    _, n_rows, d = buf.shape
    n_blk = n_rows // ROW_BLOCK
    wspec = lambda a: pl.BlockSpec((1,) + a.shape[1:], lambda i, be, nu, nv: (be[i], 0, 0))
    rows = pl.BlockSpec((2, ROW_BLOCK, d), lambda i, be, nu, nv: (0, i, 0))
    return pl.pallas_call(
        _expert_kernel,
        grid_spec=pltpu.PrefetchScalarGridSpec(
            num_scalar_prefetch=3,
            grid=(n_blk,),
            in_specs=[rows, wspec(wg), wspec(bg), wspec(wu), wspec(bu), wspec(wd), wspec(bd)],
            out_specs=rows,
            scratch_shapes=[pltpu.VMEM(wg.shape[1:], BF16), pltpu.VMEM(wu.shape[1:], BF16),
                            pltpu.VMEM(wd.shape[1:], BF16)],
        ),
        out_shape=jax.ShapeDtypeStruct((2, n_rows, d), U32),
        compiler_params=_cparams(("arbitrary",)),
        name="experts",
    )(blk_e, n_used, n_valid, buf, wg, bg, wu, bu, wd, bd)


def _combine_kernel(h1_ref, y_ref, gate_ref, lg_ref, lb_ref, o_ref):
    gates = gate_ref[...]
    parts = None
    for k in range(TOP_K):
        g = gates[:, k:k + 1]
        ys = [g * c.astype(F32) for c in _unpack_halves((y_ref[0, k], y_ref[1, k]))]
        parts = ys if parts is None else [p + y for p, y in zip(parts, ys)]
    f = jnp.concatenate(parts, axis=1)
    o_ref[...] = _layer_norm(DN_ALPHA * h1_ref[...] + f, lg_ref[...], lb_ref[...])


def _combine(h1, yg, gates, lg, lb, tm):
    t, d = h1.shape
    full = lambda a: pl.BlockSpec(a.shape, lambda i: (0,) * a.ndim)
    return pl.pallas_call(
        _combine_kernel,
        grid=(t // tm,),
        in_specs=[pl.BlockSpec((tm, d), lambda i: (i, 0)),
                  pl.BlockSpec((2, TOP_K, tm, ROW_WORDS), lambda i: (0, 0, i, 0)),
                  pl.BlockSpec((tm, TOP_K), lambda i: (i, 0)), full(lg), full(lb)],
        out_specs=pl.BlockSpec((tm, d), lambda i: (i, 0)),
        out_shape=jax.ShapeDtypeStruct((t, d), F32),
        compiler_params=_cparams(("parallel",)),
        name="combine",
    )(h1, yg, gates, lg, lb)


def _rope_tables(positions):
    half = QK_ROPE // 2
    freqs = ROPE_THETA ** (-jnp.arange(half, dtype=F32) * 2.0 / QK_ROPE)
    ang = positions.astype(F32)[:, None] * freqs[None, :]
    pad = jnp.zeros((positions.shape[0], LANES - QK_ROPE), F32)
    cos, sin = jnp.cos(ang), jnp.sin(ang)
    return jnp.concatenate([cos, cos, pad], axis=1), jnp.concatenate([sin, sin, pad], axis=1)


def _rope_cols(w):
    half = QK_ROPE // 2
    x1, x2 = w[:, :half], w[:, half:]
    pad = jnp.zeros((w.shape[0], LANES - QK_ROPE), w.dtype)
    return jnp.concatenate([x1, x2, pad], axis=1), jnp.concatenate([-x2, x1, pad], axis=1)


def _layout_weights(w_in, q_norm_g, w_uq, kv_norm_g, w_ukv):
    c_q = w_in[:, 0:Q_LORA]
    c_kv = w_in[:, Q_LORA:Q_LORA + KV_LORA]
    o = Q_LORA + KV_LORA
    k_r = w_in[:, o:o + QK_ROPE]
    o += QK_ROPE
    sb_q = w_in[:, o:o + SB_W] * (SB_DIM ** -0.5)
    sb_k = w_in[:, o + SB_W:o + 2 * SB_W]
    sb_v = w_in[:, o + 2 * SB_W:o + 3 * SB_W]
    kr, kr_rot = _rope_cols(k_r)
    w1 = jnp.concatenate([c_q, c_kv, sb_q, sb_k, kr, kr_rot], axis=1).astype(BF16)
    wsv_t = sb_v.T.astype(BF16)

    uq = w_uq.reshape(Q_LORA, MLA_HEADS, QK_NOPE + QK_ROPE)
    nope = uq[:, :, :QK_NOPE].reshape(Q_LORA, MLA_HEADS * QK_NOPE)
    ropes = [_rope_cols(uq[:, h, QK_NOPE:]) for h in range(MLA_HEADS)]
    wq = jnp.concatenate([nope] + [r[0] for r in ropes] + [r[1] for r in ropes], axis=1).astype(BF16)

    ukv = w_ukv.reshape(KV_LORA, MLA_HEADS, QK_NOPE + V_DIM)
    wk = ukv[:, :, :QK_NOPE].reshape(KV_LORA, -1).astype(BF16)
    wv_t = ukv[:, :, QK_NOPE:].reshape(KV_LORA, -1).T.astype(BF16)
    return (w1, wsv_t, q_norm_g.reshape(1, -1), kv_norm_g.reshape(1, -1), wq, wk, wv_t)


def kernel(x, meta_tokens, w_in, q_norm_g, w_uq, kv_norm_g, w_ukv, mla_out_g, sb_out_g, w_o, ln1_g, ln1_b,
           w_router, b_router, w_gate, b_gate, w_up, b_up, w_down, b_down, ln2_g, ln2_b):
    b, seq, d = x.shape
    t = b * seq
    tq_mla = min(1024, seq)
    tk_mla = min(512, seq)
    tq_sb = min(256, seq)
    tm_proj = min(512, seq)
    tm = min(512, seq)
    tt = min(512, t)
    row = lambda v: v.reshape(1, -1)

    w = _layout_weights(w_in[0], q_norm_g[0], w_uq[0], kv_norm_g[0], w_ukv[0])
    cos_x, sin_x = _rope_tables(N_META + jnp.arange(seq))
    cos_m, sin_m = _rope_tables(jnp.arange(META_ROWS))
    meta = jnp.zeros((1, META_ROWS, d), x.dtype).at[0, :N_META].set(meta_tokens.astype(x.dtype))
    q_mla, k_mla, vt_mla, sb_q, sb_k, sb_vt = _project(x, w, cos_x, sin_x, tm_proj, tk_mla, tq_sb)
    _, km_meta, vtm_meta, _, sk_meta, svt_meta = _project(meta, w, cos_m, sin_m, META_ROWS, META_ROWS, META_ROWS)
    o_mla = _mla_attention(q_mla, k_mla, vt_mla, km_meta, vtm_meta, tq_mla, tk_mla)
    o_sb = _sb_attention(sb_q, sb_k, sb_vt, sk_meta, svt_meta, tq_sb)

    h1, h1p, logits_t = _post(x.reshape(t, d), o_mla.reshape(t, MLA_W), o_sb.reshape(t, SB_W),
                              row(mla_out_g[0]), row(sb_out_g[0]), w_o[0].astype(BF16),
                              row(ln1_g[0]), row(ln1_b[0]), w_router[0].T, b_router[0].reshape(-1, 1), tm)

    top_i, gates, rank, cnt = _route(logits_t, tt)
    counts = cnt[:, 0].astype(I32)
    padded = (counts + ROW_BLOCK - 1) // ROW_BLOCK * ROW_BLOCK
    p_end = jnp.cumsum(padded)
    p_start = p_end - padded
    expert_ids = jnp.arange(N_EXPERTS, dtype=I32)[:, None, None]
    dest = jnp.sum(jnp.where(top_i[None] == expert_ids, p_start[:, None, None], 0), axis=0) + rank
    n_blk = (t * TOP_K + N_EXPERTS * (ROW_BLOCK - 1) + ROW_BLOCK - 1) // ROW_BLOCK
    blk_e = jnp.minimum(jnp.sum(jnp.arange(n_blk)[:, None] * ROW_BLOCK >= p_end[None, :], axis=1),
                        N_EXPERTS - 1).astype(I32)
    n_used = (p_end[-1:] // ROW_BLOCK).astype(I32)

    blk_start = jnp.arange(n_blk, dtype=I32) * ROW_BLOCK
    row_end = jnp.sum(jnp.where(blk_e[None, :] == expert_ids[:, :, 0], (p_start + counts)[:, None], 0), axis=0)
    n_valid = jnp.clip(row_end - blk_start, 0, ROW_BLOCK).astype(I32)

    n_rows = n_blk * ROW_BLOCK
    dest2 = jnp.concatenate([dest, dest + n_rows], axis=1)
    bufp = _sc_scatter_rows(h1p.reshape(2 * t, ROW_WORDS), dest2, 2 * n_rows).reshape(2, n_rows, ROW_WORDS)
    yp = _experts(blk_e, n_used, n_valid, bufp,
                  w_gate[0], b_gate[0].reshape(N_EXPERTS, 1, D_FF),
                  w_up[0], b_up[0].reshape(N_EXPERTS, 1, D_FF),
                  w_down[0], b_down[0].reshape(N_EXPERTS, 1, d))
    src2 = jnp.stack([dest, dest + n_rows]).reshape(-1)
    ygp = _sc_gather_rows(yp.reshape(2 * n_rows, ROW_WORDS), src2).reshape(2, TOP_K, t, ROW_WORDS)

    out = _combine(h1, ygp, gates.T, row(ln2_g[0]), row(ln2_b[0]), tm)
    return out.reshape(b, seq, d)
```

```python
import functools

import jax
import jax.numpy as jnp
from jax import lax
from jax.experimental import pallas as pl
from jax.experimental.pallas import tpu as pltpu
from jax.experimental.pallas import tpu_sc as plsc

F32 = jnp.float32
BF16 = jnp.bfloat16
I32 = jnp.int32
U32 = jnp.uint32

D_MODEL = 1024
N_META = 16
MLA_HEADS = 4
QK_NOPE = 128
QK_ROPE = 64
V_DIM = 128
Q_LORA = 256
KV_LORA = 256
ROPE_THETA = 10000.0
SB_HEADS = 8
SB_DIM = 64
MLA_W = MLA_HEADS * V_DIM
SB_W = SB_HEADS * SB_DIM
N_EXPERTS = 32
TOP_K = 4
D_FF = 1024
SWIGLU_LIMIT = 7.0
SWIGLU_ALPHA = 1.702
DEPTH = 1
DN_ALPHA = (2 * DEPTH) ** 0.25
LN_EPS = 1e-5
RMS_EPS = 1e-6
NEG_INF = -1e30

LANES = 128
META_ROWS = 128
QK_PAD = 2 * LANES
ROW_BLOCK = 512
LOG2_E = 1.4426950408889634
ROW_WORDS = D_MODEL // 4
MLA_GROUP = 4
SB_GROUP = 4
SC_WINDOW = 128
VMEM_LIMIT = 56 * 1024 * 1024
SB_DEAD = -105.0


def _cparams(sem):
    return pltpu.CompilerParams(dimension_semantics=sem, vmem_limit_bytes=VMEM_LIMIT)


def _rms(x, g):
    return x * lax.rsqrt(jnp.mean(x * x, axis=-1, keepdims=True) + RMS_EPS) * g


def _layer_norm(x, g, b):
    mu = jnp.mean(x, axis=-1, keepdims=True)
    xc = x - mu
    var = jnp.mean(xc * xc, axis=-1, keepdims=True)
    return xc * lax.rsqrt(var + LN_EPS) * g + b


def _dot(a, b):
    return jnp.dot(a, b, preferred_element_type=F32)


def _dot_nt(a, b):
    return lax.dot_general(a, b, (((1,), (1,)), ((), ())), preferred_element_type=F32)


def _bf16_bits(x):
    bits = pltpu.bitcast(x, U32)
    return (bits + jnp.uint32(0x7FFF) + ((bits >> 16) & jnp.uint32(1))) & jnp.uint32(0xFFFF0000)


def _pack_rows(x):
    n = x.shape[1] // 2
    return _bf16_bits(x[:, n:]) | (_bf16_bits(x[:, :n]) >> 16)


def _unpack_rows(p):
    lo = pltpu.bitcast(p << 16, F32)
    hi = pltpu.bitcast(p & jnp.uint32(0xFFFF0000), F32)
    return lo.astype(BF16), hi.astype(BF16)


def _store_halves(ref, packed):
    ref[0] = packed[:, :ROW_WORDS]
    ref[1] = packed[:, ROW_WORDS:]


def _unpack_halves(ref_halves):
    lo0, hi0 = _unpack_rows(ref_halves[0])
    lo1, hi1 = _unpack_rows(ref_halves[1])
    return lo0, lo1, hi0, hi1


def _proj_kernel(x_ref, w1_ref, wsv_ref, qg_ref, kvg_ref, wq_ref, wk_ref, wv_ref, cos_ref, sin_ref,
                 qm_ref, km_ref, vmt_ref, sq_ref, sk_ref, svt_ref, *, tm, tkm, tks):
    xb = x_ref[0].astype(BF16)
    p = _dot(xb, w1_ref[...])
    sq_ref[0] = p[:, 512:1024].astype(BF16)
    sk_ref[0] = p[:, 1024:1536].astype(BF16)
    svt = _dot_nt(wsv_ref[...], xb).astype(BF16)
    for j in range(tm // tks):
        svt_ref[0, j] = svt[:, j * tks:(j + 1) * tks]
    cos = cos_ref[...]
    sin = sin_ref[...]
    k_rope = (p[:, 1536:1664] * cos + p[:, 1664:1792] * sin).astype(BF16)
    nq = _rms(p[:, 0:256], qg_ref[...]).astype(BF16)
    nkv = _rms(p[:, 256:512], kvg_ref[...]).astype(BF16)
    qq = _dot(nq, wq_ref[...])
    kn = _dot(nkv, wk_ref[...])
    vmt = _dot_nt(wv_ref[...], nkv).astype(BF16)
    for j in range(tm // tkm):
        vmt_ref[0, j] = vmt[:, j * tkm:(j + 1) * tkm]
    scale = (QK_NOPE + QK_ROPE) ** -0.5 * LOG2_E
    for h in range(MLA_HEADS):
        lo = h * QK_PAD
        qm_ref[0, :, lo:lo + LANES] = (qq[:, h * LANES:(h + 1) * LANES] * scale).astype(BF16)
        q_rope = qq[:, 512 + h * LANES:512 + (h + 1) * LANES] * cos + qq[:, 1024 + h * LANES:1024 + (h + 1) * LANES] * sin
        qm_ref[0, :, lo + LANES:lo + QK_PAD] = (q_rope * scale).astype(BF16)
        km_ref[0, :, lo:lo + LANES] = kn[:, h * LANES:(h + 1) * LANES].astype(BF16)
        km_ref[0, :, lo + LANES:lo + QK_PAD] = k_rope


def _project(x, w, cos, sin, tm, tkm, tks):
    b, l, d = x.shape
    full = lambda a: pl.BlockSpec(a.shape, lambda bi, i: (0,) * a.ndim)
    tok = lambda width: pl.BlockSpec((1, tm, width), lambda bi, i: (bi, i, 0))
    tab = pl.BlockSpec((tm, LANES), lambda bi, i: (i, 0))
    vt = lambda width, tk: pl.BlockSpec((1, tm // tk, width, tk), lambda bi, i: (bi, i, 0, 0))
    qk_w = MLA_HEADS * QK_PAD
    return pl.pallas_call(
        functools.partial(_proj_kernel, tm=tm, tkm=tkm, tks=tks),
        grid=(b, l // tm),
        in_specs=[tok(d)] + [full(a) for a in w] + [tab, tab],
        out_specs=[tok(qk_w), tok(qk_w), vt(MLA_W, tkm), tok(SB_W), tok(SB_W), vt(SB_W, tks)],
        out_shape=[jax.ShapeDtypeStruct((b, l, qk_w), BF16), jax.ShapeDtypeStruct((b, l, qk_w), BF16),
                   jax.ShapeDtypeStruct((b, l // tkm, MLA_W, tkm), BF16),
                   jax.ShapeDtypeStruct((b, l, SB_W), BF16), jax.ShapeDtypeStruct((b, l, SB_W), BF16),
                   jax.ShapeDtypeStruct((b, l // tks, SB_W, tks), BF16)],
        compiler_params=_cparams(("parallel", "parallel")),
        name="proj",
    )(x, *w, cos, sin)


def _mla_kernel(q_ref, k_ref, vt_ref, kmeta_ref, vtmeta_ref, o_ref, m_sc, l_sc, acc_sc, *, tq, tk):
    qi = pl.program_id(2)
    heads = range(MLA_GROUP)
    q = [q_ref[0, :, h * QK_PAD:(h + 1) * QK_PAD] for h in heads]

    for h in heads:
        s = _dot_nt(kmeta_ref[0, :, h * QK_PAD:(h + 1) * QK_PAD], q[h])
        key = lax.broadcasted_iota(I32, s.shape, 0)
        s = jnp.where(key < N_META, s, NEG_INF)
        m0 = jnp.max(s, axis=0, keepdims=True)
        p0 = jnp.exp2(s - m0)
        m_sc[h] = m0
        l_sc[h] = jnp.sum(p0, axis=0, keepdims=True)
        acc_sc[h] = _dot(vtmeta_ref[0, 0, h * V_DIM:(h + 1) * V_DIM, :], p0.astype(BF16))

    def update(s, ki):
        m_prev = [m_sc[h] for h in heads]
        m_new = [jnp.maximum(m_prev[h], jnp.max(s[h], axis=0, keepdims=True)) for h in heads]
        p = [jnp.exp2(s[h] - m_new[h]) for h in heads]
        for h in heads:
            alpha = jnp.exp2(m_prev[h] - m_new[h])
            l_sc[h] = alpha * l_sc[h] + jnp.sum(p[h], axis=0, keepdims=True)
            acc_sc[h] = alpha * acc_sc[h] + _dot(vt_ref[0, ki, h * V_DIM:(h + 1) * V_DIM, :], p[h].astype(BF16))
            m_sc[h] = m_new[h]

    def scores(ki):
        start = pl.multiple_of(ki * tk, tk)
        return tuple(_dot_nt(k_ref[0, pl.ds(start, tk), h * QK_PAD:(h + 1) * QK_PAD], q[h]) for h in heads)

    def body(ki, carry):
        update(scores(ki), ki)
        return carry

    n_diag = tq // tk
    first_diag = qi * n_diag
    lax.fori_loop(0, first_diag, body, 0)
    s = scores(first_diag)
    key = lax.broadcasted_iota(I32, (tk, tq), 0)
    qry = lax.broadcasted_iota(I32, (tk, tq), 1)
    for j in range(n_diag):
        s_next = scores(first_diag + j + 1) if j + 1 < n_diag else None
        update([jnp.where(key + j * tk <= qry, s[h], NEG_INF) for h in heads], first_diag + j)
        s = s_next
    for h in heads:
        o_ref[0, :, h * V_DIM:(h + 1) * V_DIM] = (acc_sc[h] / l_sc[h]).T


def _mla_attention(q, k, vt, kmeta, vtmeta, tq, tk):
    b, l, _ = q.shape
    g = MLA_GROUP
    return pl.pallas_call(
        functools.partial(_mla_kernel, tq=tq, tk=tk),
        grid=(b, MLA_HEADS // g, l // tq),
        in_specs=[
            pl.BlockSpec((1, tq, g * QK_PAD), lambda bi, h, i: (bi, i, h)),
            pl.BlockSpec((1, l, g * QK_PAD), lambda bi, h, i: (bi, 0, h), pipeline_mode=pl.Buffered(1)),
            pl.BlockSpec((1, l // tk, g * V_DIM, tk), lambda bi, h, i: (bi, 0, h, 0), pipeline_mode=pl.Buffered(1)),
            pl.BlockSpec((1, META_ROWS, g * QK_PAD), lambda bi, h, i: (0, 0, h)),
            pl.BlockSpec((1, 1, g * V_DIM, META_ROWS), lambda bi, h, i: (0, 0, h, 0)),
        ],
        out_specs=pl.BlockSpec((1, tq, g * V_DIM), lambda bi, h, i: (bi, i, h)),
        out_shape=jax.ShapeDtypeStruct((b, l, MLA_W), F32),
        scratch_shapes=[pltpu.VMEM((g, 1, tq), F32), pltpu.VMEM((g, 1, tq), F32), pltpu.VMEM((g, V_DIM, tq), F32)],
        compiler_params=_cparams(("parallel", "parallel", "arbitrary")),
        name="mla",
    )(q, k, vt, kmeta, vtmeta)


def _sb_kernel(q_ref, k_ref, vt_ref, kmeta_ref, vtmeta_ref, o_ref, c_sc, acc_sc, *, tq):
    qi = pl.program_id(2)
    chains = [(p, h) for p in range(SB_GROUP) for h in range(2)]
    lane = lax.broadcasted_iota(I32, (tq, LANES), 1)
    q_heads = []
    for p, h in chains:
        q = q_ref[0, :, p * LANES:(p + 1) * LANES]
        keep = lane < SB_DIM if h == 0 else lane >= SB_DIM
        q_heads.append(jnp.where(keep, q, jnp.zeros_like(q)))

    def later_matrix(n):
        r = lax.broadcasted_iota(I32, (n, n), 0)
        c = lax.broadcasted_iota(I32, (n, n), 1)
        return jnp.where(c > r, 1.0, 0.0).astype(BF16)

    c_sc[...] = jnp.zeros_like(c_sc)
    acc_sc[...] = jnp.zeros_like(acc_sc)

    def block(kblk, vt, valid, u):
        z = [_dot_nt(kblk(p), q_heads[n]) for n, (p, h) in enumerate(chains)]
        log_beta, log_keep = [], []
        for n in range(len(chains)):
            softplus = jnp.maximum(z[n], 0.0) + jnp.log(1.0 + jnp.exp(-jnp.abs(z[n])))
            lk = -softplus
            if valid is not None:
                lk = jnp.where(valid, lk, 0.0)
            log_keep.append(lk)
            log_beta.append(z[n] - softplus)
        later = []
        for n in range(len(chains)):
            hi = log_keep[n].astype(BF16)
            lo = (log_keep[n] - hi.astype(F32)).astype(BF16)
            later.append(_dot(u, hi) + _dot(u, lo) + c_sc[n])
        for n, (p, h) in enumerate(chains):
            w = jnp.exp(log_beta[n] + later[n])
            if valid is not None:
                w = jnp.where(valid, w, 0.0)
            acc_sc[n] += _dot(vt(p), w.astype(BF16))
            c_sc[n] = later[n][0:1, :] + log_keep[n][0:1, :]

    def keys_at(ki):
        st = pl.multiple_of(ki * tq, tq)
        return (lambda p: k_ref[0, pl.ds(st, tq), p * LANES:(p + 1) * LANES],
                lambda p: vt_ref[0, ki, p * LANES:(p + 1) * LANES, :])

    u_blk = later_matrix(tq)
    key = lax.broadcasted_iota(I32, (tq, tq), 0)
    qry = lax.broadcasted_iota(I32, (tq, tq), 1)
    block(*keys_at(qi), key < qry, u_blk)

    def cond(carry):
        i, cm = carry
        return jnp.logical_and(i < qi, cm > SB_DEAD)

    def body(carry):
        i, _ = carry
        block(*keys_at(qi - 1 - i), None, u_blk)
        return i + 1, jnp.max(c_sc[...])

    _, cm = lax.while_loop(cond, body, (jnp.int32(0), jnp.max(c_sc[...])))

    @pl.when(cm > SB_DEAD)
    def _():
        mkey = lax.broadcasted_iota(I32, (META_ROWS, tq), 0)
        block(lambda p: kmeta_ref[0, :, p * LANES:(p + 1) * LANES],
              lambda p: vtmeta_ref[0, 0, p * LANES:(p + 1) * LANES, :], mkey < N_META, later_matrix(META_ROWS))

    feat = lax.broadcasted_iota(I32, (LANES, tq), 0)
    for p in range(SB_GROUP):
        o_ref[0, :, p * LANES:(p + 1) * LANES] = jnp.where(feat < SB_DIM, acc_sc[2 * p], acc_sc[2 * p + 1]).T


def _sb_attention(q, k, vt, kmeta, vtmeta, tq):
    b, l, _ = q.shape
    g = SB_GROUP
    gw = g * LANES
    return pl.pallas_call(
        functools.partial(_sb_kernel, tq=tq),
        grid=(b, SB_W // gw, l // tq),
        in_specs=[
            pl.BlockSpec((1, tq, gw), lambda bi, h, i: (bi, i, h)),
            pl.BlockSpec((1, l, gw), lambda bi, h, i: (bi, 0, h), pipeline_mode=pl.Buffered(1)),
            pl.BlockSpec((1, l // tq, gw, tq), lambda bi, h, i: (bi, 0, h, 0), pipeline_mode=pl.Buffered(1)),
            pl.BlockSpec((1, META_ROWS, gw), lambda bi, h, i: (0, 0, h)),
            pl.BlockSpec((1, 1, gw, META_ROWS), lambda bi, h, i: (0, 0, h, 0)),
        ],
        out_specs=pl.BlockSpec((1, tq, gw), lambda bi, h, i: (bi, i, h)),
        out_shape=jax.ShapeDtypeStruct((b, l, SB_W), F32),
        scratch_shapes=[pltpu.VMEM((2 * g, 1, tq), F32), pltpu.VMEM((2 * g, LANES, tq), F32)],
        compiler_params=_cparams(("parallel", "parallel", "arbitrary")),
        name="sb",
    )(q, k, vt, kmeta, vtmeta)


def _post_kernel(x_ref, om_ref, os_ref, g1_ref, g2_ref, wo_ref, lg_ref, lb_ref, wr_ref, br_ref,
                 h1_ref, h1p_ref, lt_ref):
    n1 = _rms(om_ref[...], g1_ref[...]).astype(BF16)
    n2 = _rms(os_ref[...], g2_ref[...]).astype(BF16)
    a = _dot(n1, wo_ref[0:MLA_W, :]) + _dot(n2, wo_ref[MLA_W:MLA_W + SB_W, :])
    h1 = _layer_norm(DN_ALPHA * x_ref[...] + a, lg_ref[...], lb_ref[...])
    h1_ref[...] = h1
    _store_halves(h1p_ref, _pack_rows(h1))
    lt_ref[...] = lax.dot_general(wr_ref[...], h1, (((1,), (1,)), ((), ())),
                                  precision=lax.Precision.HIGHEST, preferred_element_type=F32) + br_ref[...]


def _post(x2, om, osb, g1, g2, wo, lg, lb, wr_t, br, tm):
    t, d = x2.shape
    full = lambda a: pl.BlockSpec(a.shape, lambda i: (0,) * a.ndim)
    tok = lambda w: pl.BlockSpec((tm, w), lambda i: (i, 0))
    return pl.pallas_call(
        _post_kernel,
        grid=(t // tm,),
        in_specs=[tok(d), tok(MLA_W), tok(SB_W), full(g1), full(g2), full(wo), full(lg), full(lb), full(wr_t), full(br)],
        out_specs=[tok(d), pl.BlockSpec((2, tm, ROW_WORDS), lambda i: (0, i, 0)),
                   pl.BlockSpec((N_EXPERTS, tm), lambda i: (0, i))],
        out_shape=[jax.ShapeDtypeStruct((t, d), F32), jax.ShapeDtypeStruct((2, t, ROW_WORDS), U32),
                   jax.ShapeDtypeStruct((N_EXPERTS, t), F32)],
        compiler_params=_cparams(("parallel",)),
        name="post",
    )(x2, om, osb, g1, g2, wo, lg, lb, wr_t, br)


def _route_kernel(lt_ref, ti_ref, gate_ref, rank_ref, cnt_ref, carry_sc, *, tt):
    @pl.when(pl.program_id(0) == 0)
    def _():
        carry_sc[...] = jnp.zeros_like(carry_sc)

    logits = lt_ref[...]
    eidx = lax.broadcasted_iota(I32, logits.shape, 0)
    sels, vals = [], []
    for k in range(TOP_K):
        mx = jnp.max(logits, axis=0, keepdims=True)
        idx = jnp.min(jnp.where(logits == mx, eidx, N_EXPERTS), axis=0, keepdims=True)
        sel = eidx == idx
        logits = jnp.where(sel, -jnp.inf, logits)
        ti_ref[k:k + 1, :] = idx
        sels.append(sel)
        vals.append(mx)
    exps = [jnp.exp(v - vals[0]) for v in vals]
    denom = exps[0] + exps[1] + exps[2] + exps[3]
    for k in range(TOP_K):
        gate_ref[k:k + 1, :] = exps[k] / denom

    chosen = sum(jnp.where(s, 1.0, 0.0) for s in sels)
    r = lax.broadcasted_iota(I32, (tt, tt), 0)
    c = lax.broadcasted_iota(I32, (tt, tt), 1)
    before = jnp.where(r < c, 1.0, 0.0).astype(BF16)
    rank = _dot(chosen.astype(BF16), before) + carry_sc[...]
    for k in range(TOP_K):
        rank_ref[k:k + 1, :] = jnp.sum(jnp.where(sels[k], rank, 0.0), axis=0, keepdims=True).astype(I32)
    carry_sc[...] += jnp.sum(chosen, axis=1, keepdims=True)
    cnt_ref[...] = jnp.broadcast_to(carry_sc[...], cnt_ref.shape)


def _route(lt, tt):
    e, t = lt.shape
    tok = pl.BlockSpec((TOP_K, tt), lambda i: (0, i))
    return pl.pallas_call(
        functools.partial(_route_kernel, tt=tt),
        grid=(t // tt,),
        in_specs=[pl.BlockSpec((e, tt), lambda i: (0, i))],
        out_specs=[tok, tok, tok, pl.BlockSpec((e, LANES), lambda i: (0, 0))],
        out_shape=[jax.ShapeDtypeStruct((TOP_K, t), I32), jax.ShapeDtypeStruct((TOP_K, t), F32),
                   jax.ShapeDtypeStruct((TOP_K, t), I32), jax.ShapeDtypeStruct((e, LANES), F32)],
        scratch_shapes=[pltpu.VMEM((e, 1), F32)],
        compiler_params=_cparams(("arbitrary",)),
        name="route",
    )(lt)


def _sc_mesh():
    return plsc.VectorSubcoreMesh(core_axis_name="c", subcore_axis_name="s")


def _sc_gather_rows(x, idx):
    m = idx.shape[0]
    w = x.shape[1]

    @pl.kernel(out_type=jax.ShapeDtypeStruct((m, w), x.dtype), mesh=_sc_mesh())
    def gather(x_hbm, i_hbm, o_hbm):
        def body(i_vmem, o_vmem):
            pltpu.sync_copy(x_hbm.at[i_vmem.at[0]], o_vmem)

        pltpu.emit_pipeline(
            body, grid=(m // SC_WINDOW,),
            in_specs=[pl.BlockSpec((1, SC_WINDOW), lambda i: (0, i))],
            out_specs=[pl.BlockSpec((SC_WINDOW, w), lambda i: (i, 0))],
            core_axis_name=("c", "s"), dimension_semantics=(pltpu.PARALLEL,),
        )(i_hbm, o_hbm)

    return gather(x, idx.reshape(1, m))


def _sc_scatter_rows(x, dest, n_rows):
    kk, t = dest.shape
    w = x.shape[1]
    nt = t // SC_WINDOW

    @pl.kernel(out_type=jax.ShapeDtypeStruct((n_rows, w), x.dtype), mesh=_sc_mesh(), scratch_types=[])
    def scatter(x_hbm, i_hbm, o_hbm):
        def body(x_vmem, i_vmem):
            pltpu.sync_copy(x_vmem, o_hbm.at[i_vmem.at[0]])

        pltpu.emit_pipeline(
            body, grid=(kk * nt,),
            in_specs=[pl.BlockSpec((SC_WINDOW, w), lambda i: (i % nt, 0)),
                      pl.BlockSpec((1, SC_WINDOW), lambda i: (i // nt, i % nt))],
            out_specs=[],
            core_axis_name=("c", "s"), dimension_semantics=(pltpu.PARALLEL,),
        )(x_hbm, i_hbm)

    return scatter(x, dest)


def _expert_kernel(be_ref, nu_ref, nv_ref, x_ref, wg_ref, bg_ref, wu_ref, bu_ref, wd_ref, bd_ref, y_ref,
                   wg_sc, wu_sc, wd_sc):
    i = pl.program_id(0)
    active = i < nu_ref[0]

    @pl.when(jnp.logical_and(active, jnp.logical_or(i == 0, be_ref[i] != be_ref[jnp.maximum(i - 1, 0)])))
    def _():
        wg_sc[...] = wg_ref[0].astype(BF16)
        wu_sc[...] = wu_ref[0].astype(BF16)
        wd_sc[...] = wd_ref[0].astype(BF16)

    @pl.when(active)
    def _():
        live = lax.broadcasted_iota(I32, (ROW_BLOCK, ROW_WORDS), 0) < nv_ref[i]
        zero = jnp.zeros((ROW_BLOCK, ROW_WORDS), U32)
        lo0, lo1, hi0, hi1 = _unpack_halves((jnp.where(live, x_ref[0], zero), jnp.where(live, x_ref[1], zero)))
        xs = (jnp.concatenate([lo0, lo1], axis=1), jnp.concatenate([hi0, hi1], axis=1))
        cw = D_MODEL // len(xs)

        def proj(w_sc, b_ref):
            acc = b_ref[0]
            for j, xj in enumerate(xs):
                acc = acc + _dot(xj, w_sc[j * cw:(j + 1) * cw, :])
            return acc

        g = jnp.minimum(proj(wg_sc, bg_ref), SWIGLU_LIMIT)
        u = jnp.clip(proj(wu_sc, bu_ref), -SWIGLU_LIMIT, SWIGLU_LIMIT)
        a = (u + 1.0) * (g * (1.0 / (1.0 + jnp.exp(-SWIGLU_ALPHA * g))))
        _store_halves(y_ref, _pack_rows(_dot(a.astype(BF16), wd_sc[...]) + bd_ref[0]))

    @pl.when(i >= nu_ref[0])
    def _():
        y_ref[...] = jnp.zeros_like(y_ref)


def _experts(blk_e, n_used, n_valid, buf, wg, bg, wu, bu, wd, bd):
    _, n_rows, d = buf.shape
    n_blk = n_rows // ROW_BLOCK
    wspec = lambda a: pl.BlockSpec((1,) + a.shape[1:], lambda i, be, nu, nv: (be[i], 0, 0))
    rows = pl.BlockSpec((2, ROW_BLOCK, d), lambda i, be, nu, nv: (0, i, 0))
    return pl.pallas_call(
        _expert_kernel,
        grid_spec=pltpu.PrefetchScalarGridSpec(
            num_scalar_prefetch=3,
            grid=(n_blk,),
            in_specs=[rows, wspec(wg), wspec(bg), wspec(wu), wspec(bu), wspec(wd), wspec(bd)],
            out_specs=rows,
            scratch_shapes=[pltpu.VMEM(wg.shape[1:], BF16), pltpu.VMEM(wu.shape[1:], BF16),
                            pltpu.VMEM(wd.shape[1:], BF16)],
        ),
        out_shape=jax.ShapeDtypeStruct((2, n_rows, d), U32),
        compiler_params=_cparams(("arbitrary",)),
        name="experts",
    )(blk_e, n_used, n_valid, buf, wg, bg, wu, bu, wd, bd)


def _combine_kernel(h1_ref, y_ref, gate_ref, lg_ref, lb_ref, o_ref):
    gates = gate_ref[...]
    parts = None
    for k in range(TOP_K):
        g = gates[:, k:k + 1]
        ys = [g * c.astype(F32) for c in _unpack_halves((y_ref[0, k], y_ref[1, k]))]
        parts = ys if parts is None else [p + y for p, y in zip(parts, ys)]
    f = jnp.concatenate(parts, axis=1)
    o_ref[...] = _layer_norm(DN_ALPHA * h1_ref[...] + f, lg_ref[...], lb_ref[...])


def _combine(h1, yg, gates, lg, lb, tm):
    t, d = h1.shape
    full = lambda a: pl.BlockSpec(a.shape, lambda i: (0,) * a.ndim)
    return pl.pallas_call(
        _combine_kernel,
        grid=(t // tm,),
        in_specs=[pl.BlockSpec((tm, d), lambda i: (i, 0)),
                  pl.BlockSpec((2, TOP_K, tm, ROW_WORDS), lambda i: (0, 0, i, 0)),
                  pl.BlockSpec((tm, TOP_K), lambda i: (i, 0)), full(lg), full(lb)],
        out_specs=pl.BlockSpec((tm, d), lambda i: (i, 0)),
        out_shape=jax.ShapeDtypeStruct((t, d), F32),
        compiler_params=_cparams(("parallel",)),
        name="combine",
    )(h1, yg, gates, lg, lb)


def _rope_tables(positions):
    half = QK_ROPE // 2
    freqs = ROPE_THETA ** (-jnp.arange(half, dtype=F32) * 2.0 / QK_ROPE)
    ang = positions.astype(F32)[:, None] * freqs[None, :]
    pad = jnp.zeros((positions.shape[0], LANES - QK_ROPE), F32)
    cos, sin = jnp.cos(ang), jnp.sin(ang)
    return jnp.concatenate([cos, cos, pad], axis=1), jnp.concatenate([sin, sin, pad], axis=1)


def _rope_cols(w):
    half = QK_ROPE // 2
    x1, x2 = w[:, :half], w[:, half:]
    pad = jnp.zeros((w.shape[0], LANES - QK_ROPE), w.dtype)
    return jnp.concatenate([x1, x2, pad], axis=1), jnp.concatenate([-x2, x1, pad], axis=1)


def _layout_weights(w_in, q_norm_g, w_uq, kv_norm_g, w_ukv):
    c_q = w_in[:, 0:Q_LORA]
    c_kv = w_in[:, Q_LORA:Q_LORA + KV_LORA]
    o = Q_LORA + KV_LORA
    k_r = w_in[:, o:o + QK_ROPE]
    o += QK_ROPE
    sb_q = w_in[:, o:o + SB_W] * (SB_DIM ** -0.5)
    sb_k = w_in[:, o + SB_W:o + 2 * SB_W]
    sb_v = w_in[:, o + 2 * SB_W:o + 3 * SB_W]
    kr, kr_rot = _rope_cols(k_r)
    w1 = jnp.concatenate([c_q, c_kv, sb_q, sb_k, kr, kr_rot], axis=1).astype(BF16)
    wsv_t = sb_v.T.astype(BF16)

    uq = w_uq.reshape(Q_LORA, MLA_HEADS, QK_NOPE + QK_ROPE)
    nope = uq[:, :, :QK_NOPE].reshape(Q_LORA, MLA_HEADS * QK_NOPE)
    ropes = [_rope_cols(uq[:, h, QK_NOPE:]) for h in range(MLA_HEADS)]
    wq = jnp.concatenate([nope] + [r[0] for r in ropes] + [r[1] for r in ropes], axis=1).astype(BF16)

    ukv = w_ukv.reshape(KV_LORA, MLA_HEADS, QK_NOPE + V_DIM)
    wk = ukv[:, :, :QK_NOPE].reshape(KV_LORA, -1).astype(BF16)
    wv_t = ukv[:, :, QK_NOPE:].reshape(KV_LORA, -1).T.astype(BF16)
    return (w1, wsv_t, q_norm_g.reshape(1, -1), kv_norm_g.reshape(1, -1), wq, wk, wv_t)


def kernel(x, meta_tokens, w_in, q_norm_g, w_uq, kv_norm_g, w_ukv, mla_out_g, sb_out_g, w_o, ln1_g, ln1_b,
           w_router, b_router, w_gate, b_gate, w_up, b_up, w_down, b_down, ln2_g, ln2_b):
    b, seq, d = x.shape
    t = b * seq
    tq_mla = min(1024, seq)
    tk_mla = min(512, seq)
    tq_sb = min(256, seq)
    tm_proj = min(512, seq)
    tm = min(512, seq)
    tt = min(512, t)
    row = lambda v: v.reshape(1, -1)

    w = _layout_weights(w_in[0], q_norm_g[0], w_uq[0], kv_norm_g[0], w_ukv[0])
    cos_x, sin_x = _rope_tables(N_META + jnp.arange(seq))
    cos_m, sin_m = _rope_tables(jnp.arange(META_ROWS))
    meta = jnp.zeros((1, META_ROWS, d), x.dtype).at[0, :N_META].set(meta_tokens.astype(x.dtype))
    q_mla, k_mla, vt_mla, sb_q, sb_k, sb_vt = _project(x, w, cos_x, sin_x, tm_proj, tk_mla, tq_sb)
    _, km_meta, vtm_meta, _, sk_meta, svt_meta = _project(meta, w, cos_m, sin_m, META_ROWS, META_ROWS, META_ROWS)
    o_mla = _mla_attention(q_mla, k_mla, vt_mla, km_meta, vtm_meta, tq_mla, tk_mla)
    o_sb = _sb_attention(sb_q, sb_k, sb_vt, sk_meta, svt_meta, tq_sb)

    h1, h1p, logits_t = _post(x.reshape(t, d), o_mla.reshape(t, MLA_W), o_sb.reshape(t, SB_W),
                              row(mla_out_g[0]), row(sb_out_g[0]), w_o[0].astype(BF16),
                              row(ln1_g[0]), row(ln1_b[0]), w_router[0].T, b_router[0].reshape(-1, 1), tm)

    top_i, gates, rank, cnt = _route(logits_t, tt)
    counts = cnt[:, 0].astype(I32)
    padded = (counts + ROW_BLOCK - 1) // ROW_BLOCK * ROW_BLOCK
    p_end = jnp.cumsum(padded)
    p_start = p_end - padded
    expert_ids = jnp.arange(N_EXPERTS, dtype=I32)[:, None, None]
    dest = jnp.sum(jnp.where(top_i[None] == expert_ids, p_start[:, None, None], 0), axis=0) + rank
    n_blk = (t * TOP_K + N_EXPERTS * (ROW_BLOCK - 1) + ROW_BLOCK - 1) // ROW_BLOCK
    blk_e = jnp.minimum(jnp.sum(jnp.arange(n_blk)[:, None] * ROW_BLOCK >= p_end[None, :], axis=1),
                        N_EXPERTS - 1).astype(I32)
    n_used = (p_end[-1:] // ROW_BLOCK).astype(I32)

    blk_start = jnp.arange(n_blk, dtype=I32) * ROW_BLOCK
    row_end = jnp.sum(jnp.where(blk_e[None, :] == expert_ids[:, :, 0], (p_start + counts)[:, None], 0), axis=0)
    n_valid = jnp.clip(row_end - blk_start, 0, ROW_BLOCK).astype(I32)

    n_rows = n_blk * ROW_BLOCK
    dest2 = jnp.concatenate([dest, dest + n_rows], axis=1)
    bufp = _sc_scatter_rows(h1p.reshape(2 * t, ROW_WORDS), dest2, 2 * n_rows).reshape(2, n_rows, ROW_WORDS)
    yp = _experts(blk_e, n_used, n_valid, bufp,
                  w_gate[0], b_gate[0].reshape(N_EXPERTS, 1, D_FF),
                  w_up[0], b_up[0].reshape(N_EXPERTS, 1, D_FF),
                  w_down[0], b_down[0].reshape(N_EXPERTS, 1, d))
    src2 = jnp.stack([dest, dest + n_rows]).reshape(-1)
    ygp = _sc_gather_rows(yp.reshape(2 * n_rows, ROW_WORDS), src2).reshape(2, TOP_K, t, ROW_WORDS)

    out = _combine(h1, ygp, gates.T, row(ln2_g[0]), row(ln2_b[0]), tm)
    return out.reshape(b, seq, d)
```
